```python
import math
import jax, jax.numpy as jnp
from jax import lax
import numpy as np

D_MODEL = 2048
BATCH = 1
SEQ = 8192
DEPTH = 1

CHUNK = 64
Q_BLOCK = 128
FOX_HEADS = 8
FOX_HEAD_DIM = 128
DIFF_HEADS = 8
DIFF_QK_DIM = 64
DIFF_HEAD_DIM = 2 * DIFF_QK_DIM
ROPE_DIM = DIFF_QK_DIM // 4
ROPE_THETA = 500000.0
D_FF = 5504
CONV_WIDTH = 3
EPS = 1e-6
NEG_INF = -1e30
FOX_W = FOX_HEADS * FOX_HEAD_DIM
DIFF_W = DIFF_HEADS * DIFF_HEAD_DIM
MIX_W = FOX_W + DIFF_W
IN_COLS = 3 * FOX_W + FOX_HEADS + 3 * DIFF_W

kernel_name = "hybrid_fox_diffattn_convffn"


def rms_norm(x, g):
    xf = x.astype(jnp.float32)
    y = xf * lax.rsqrt(jnp.mean(xf * xf, axis=-1, keepdims=True) + EPS)
    return (y * g.astype(jnp.float32)).astype(x.dtype)


def partial_rope(x, cos, sin):
    half = ROPE_DIM // 2
    xr, xp = x[..., :ROPE_DIM], x[..., ROPE_DIM:]
    x1, x2 = xr[..., :half], xr[..., half:]
    c = cos[None, :, None, None, :].astype(x.dtype)
    s = sin[None, :, None, None, :].astype(x.dtype)
    rot = jnp.concatenate([x1 * c - x2 * s, x2 * c + x1 * s], axis=-1)
    return jnp.concatenate([rot, xp], axis=-1)


def mixer_layer(x, norm1_g, w_in, b_forget, fox_q_g, fox_k_g, diff_q_g, diff_k_g,
                lam_q1, lam_k1, lam_q2, lam_k2, diff_subln_g, w_out, layer_idx):
    B, S, _ = x.shape
    nb = S // Q_BLOCK
    lambda_init = 0.8 - 0.6 * math.exp(-0.3 * layer_idx)
    h = rms_norm(x, norm1_g)
    proj = h @ w_in
    cuts = [FOX_W, 2 * FOX_W, 3 * FOX_W, 3 * FOX_W + FOX_HEADS,
            3 * FOX_W + FOX_HEADS + DIFF_W, 3 * FOX_W + FOX_HEADS + 2 * DIFF_W]
    fq, fk, fv, flog, dq, dk, dv = jnp.split(proj, cuts, axis=-1)

    pos = jnp.arange(S, dtype=jnp.int32)

    fq = rms_norm(fq.reshape(B, S, FOX_HEADS, FOX_HEAD_DIM), fox_q_g)
    fk = rms_norm(fk.reshape(B, S, FOX_HEADS, FOX_HEAD_DIM), fox_k_g)
    fv = fv.reshape(B, S, FOX_HEADS, FOX_HEAD_DIM)
    log_f = jax.nn.log_sigmoid(flog.astype(jnp.float32) + b_forget.astype(jnp.float32))
    c = jnp.cumsum(log_f, axis=1)
    fq_blk = fq.reshape(B, nb, Q_BLOCK, FOX_HEADS, FOX_HEAD_DIM).transpose(1, 0, 3, 2, 4)
    cq_blk = c.reshape(B, nb, Q_BLOCK, FOX_HEADS).transpose(1, 0, 3, 2)
    fk_h = fk.transpose(0, 2, 1, 3)
    fv_h = fv.transpose(0, 2, 1, 3)
    ck = c.transpose(0, 2, 1)

    inv_freq = ROPE_THETA ** (-jnp.arange(0, ROPE_DIM, 2, dtype=jnp.float32) / ROPE_DIM)
    ang = pos.astype(jnp.float32)[:, None] * inv_freq[None, :]
    cos, sin = jnp.cos(ang), jnp.sin(ang)
    dq = partial_rope(rms_norm(dq.reshape(B, S, DIFF_HEADS, 2, DIFF_QK_DIM), diff_q_g), cos, sin)
    dk = partial_rope(rms_norm(dk.reshape(B, S, DIFF_HEADS, 2, DIFF_QK_DIM), diff_k_g), cos, sin)
    dv_h = dv.reshape(B, S, DIFF_HEADS, DIFF_HEAD_DIM).transpose(0, 2, 1, 3)
    dq_blk = dq.reshape(B, nb, Q_BLOCK, DIFF_HEADS, 2, DIFF_QK_DIM).transpose(1, 0, 3, 4, 2, 5)
    dk_h = dk.transpose(0, 2, 3, 1, 4)
    lam = (jnp.exp(jnp.sum(lam_q1.astype(jnp.float32) * lam_k1.astype(jnp.float32)))
           - jnp.exp(jnp.sum(lam_q2.astype(jnp.float32) * lam_k2.astype(jnp.float32)))
           + lambda_init)

    qpos_blk = pos.reshape(nb, Q_BLOCK)
    fox_scale = FOX_HEAD_DIM ** -0.5
    diff_scale = DIFF_QK_DIM ** -0.5

    def attend_block(args):
        fq_b, cq_b, dq_b, qpos = args
        s_f = jnp.einsum('bhqd,bhkd->bhqk', fq_b, fk_h).astype(jnp.float32) * fox_scale
        s_f = s_f + cq_b[..., :, None] - ck[..., None, :]
        frame_mask = pos[None, :] <= qpos[:, None]
        p_f = jax.nn.softmax(jnp.where(frame_mask, s_f, NEG_INF), axis=-1)
        o_f = jnp.einsum('bhqk,bhkd->bhqd', p_f.astype(fv_h.dtype), fv_h)
        s_d = jnp.einsum('bhcqd,bhckd->bhcqk', dq_b, dk_h).astype(jnp.float32) * diff_scale
        chunk_mask = (pos[None, :] // CHUNK) <= (qpos[:, None] // CHUNK)
        p_d = jax.nn.softmax(jnp.where(chunk_mask, s_d, NEG_INF), axis=-1)
        a_d = p_d[:, :, 0] - lam * p_d[:, :, 1]
        o_d = jnp.einsum('bhqk,bhkd->bhqd', a_d.astype(dv_h.dtype), dv_h)
        o_d = rms_norm(o_d, diff_subln_g) * (1.0 - lambda_init)
        return o_f, o_d

    o_f, o_d = lax.map(attend_block, (fq_blk, cq_blk, dq_blk, qpos_blk))
    o_f = o_f.transpose(1, 0, 3, 2, 4).reshape(B, S, FOX_W)
    o_d = o_d.transpose(1, 0, 3, 2, 4).reshape(B, S, DIFF_W)
    mixed = jnp.concatenate([o_f, o_d], axis=-1)
    return x + mixed @ w_out


def conv_ffn(x, norm2_g, w_up, conv_w, conv_b, w_down):
    h = rms_norm(x, norm2_g)
    u = h @ w_up
    u = lax.conv_general_dilated(
        u, conv_w[:, None, :].astype(u.dtype), window_strides=(1,),
        padding=[(CONV_WIDTH - 1, 0)], dimension_numbers=('NWC', 'WIO', 'NWC'),
        feature_group_count=2 * D_FF) + conv_b
    gate, val = jnp.split(u, 2, axis=-1)
    return x + (jax.nn.silu(gate) * val) @ w_down


def setup_inputs(seed: int = 0) -> dict:
    key = jax.random.key(seed)
    ks = jax.random.split(key, 20)
    f32 = jnp.float32
    nrm = lambda k, shape, s: jax.random.normal(k, shape, f32) * s
    gain = lambda k, shape: 1.0 + 0.02 * jax.random.normal(k, shape, f32)
    return {
        "x": jax.random.normal(ks[0], (BATCH, SEQ, D_MODEL), f32),
        "norm1_g": gain(ks[1], (DEPTH, D_MODEL)),
        "w_in": nrm(ks[2], (DEPTH, D_MODEL, IN_COLS), D_MODEL ** -0.5),
        "b_forget": 3.0 + 0.5 * jax.random.normal(ks[3], (DEPTH, FOX_HEADS), f32),
        "fox_q_g": gain(ks[4], (DEPTH, FOX_HEAD_DIM)),
        "fox_k_g": gain(ks[5], (DEPTH, FOX_HEAD_DIM)),
        "diff_q_g": gain(ks[6], (DEPTH, DIFF_QK_DIM)),
        "diff_k_g": gain(ks[7], (DEPTH, DIFF_QK_DIM)),
        "lam_q1": nrm(ks[8], (DEPTH, DIFF_QK_DIM), 0.1),
        "lam_k1": nrm(ks[9], (DEPTH, DIFF_QK_DIM), 0.1),
        "lam_q2": nrm(ks[10], (DEPTH, DIFF_QK_DIM), 0.1),
        "lam_k2": nrm(ks[11], (DEPTH, DIFF_QK_DIM), 0.1),
        "diff_subln_g": gain(ks[12], (DEPTH, DIFF_HEAD_DIM)),
        "w_out": nrm(ks[13], (DEPTH, MIX_W, D_MODEL), MIX_W ** -0.5),
        "norm2_g": gain(ks[14], (DEPTH, D_MODEL)),
        "w_up": nrm(ks[15], (DEPTH, D_MODEL, 2 * D_FF), D_MODEL ** -0.5),
        "conv_w": nrm(ks[16], (DEPTH, CONV_WIDTH, 2 * D_FF), CONV_WIDTH ** -0.5),
        "conv_b": nrm(ks[17], (DEPTH, 2 * D_FF), 0.02),
        "w_down": nrm(ks[18], (DEPTH, D_FF, D_MODEL), D_FF ** -0.5),
    }


def reference(x, norm1_g, w_in, b_forget, fox_q_g, fox_k_g, diff_q_g, diff_k_g,
              lam_q1, lam_k1, lam_q2, lam_k2, diff_subln_g, w_out, norm2_g,
              w_up, conv_w, conv_b, w_down):
    for l in range(DEPTH):
        x = mixer_layer(x, norm1_g[l], w_in[l], b_forget[l], fox_q_g[l], fox_k_g[l],
                        diff_q_g[l], diff_k_g[l], lam_q1[l], lam_k1[l], lam_q2[l],
                        lam_k2[l], diff_subln_g[l], w_out[l], l)
        x = conv_ffn(x, norm2_g[l], w_up[l], conv_w[l], conv_b[l], w_down[l])
    return x
```

```python
import functools
import math

import jax
import jax.numpy as jnp
from jax import lax
from jax.experimental import pallas as pl
from jax.experimental.pallas import tpu as pltpu

D_MODEL = 2048
SEQ = 8192
HEADS = 8
HEAD_DIM = 128
DIFF_QK = 64
ROPE_HALF = 8
ROPE_THETA = 500000.0
GROUP_W = HEADS * HEAD_DIM
D_FF = 5504
EPS = 1e-6
NEG = -1e30
LOG2E = 1.4426950408889634
LAMBDA_INIT = 0.8 - 0.6 * math.exp(-0.3 * 0)
CHUNK_SHIFT = 6

LANES = 128
T = 512
NT = SEQ // T
FF_PAD = 5632
TF = 512
NF = FF_PAD // TF
HALO = 16
VMEM_LIMIT = 56 * 1024 * 1024

F32 = jnp.float32
BF16 = jnp.bfloat16


def _rms_rows(x, g):
    ms = jnp.mean(x * x, axis=-1, keepdims=True)
    return x * lax.rsqrt(ms + EPS) * g


def _proj_t_kernel(x_ref, g1_ref, wt_ref, fqg_ref, dqg_ref, cos_ref, sin_ref, out_ref, h_ref):
    j = pl.program_id(1)

    @pl.when(j == 0)
    def _():
        h_ref[...] = _rms_rows(x_ref[...], g1_ref[...]).astype(BF16)

    yt = lax.dot_general(wt_ref[...], h_ref[...], (((1,), (1,)), ((), ())),
                         preferred_element_type=F32)

    @pl.when(j == 0)
    def _():
        g = fqg_ref[...] * (HEAD_DIM ** -0.5 * LOG2E)
        for h in range(HEADS):
            y = yt[h * HEAD_DIM:(h + 1) * HEAD_DIM, :]
            ms = jnp.mean(y * y, axis=0, keepdims=True)
            out_ref[h * HEAD_DIM:(h + 1) * HEAD_DIM, :] = (y * lax.rsqrt(ms + EPS) * g).astype(BF16)

    @pl.when(j == 2)
    def _():
        g = dqg_ref[...] * (DIFF_QK ** -0.5 * LOG2E)
        c = cos_ref[...]
        s = sin_ref[...]
        for h in range(HEADS):
            parts = []
            for p in range(2):
                lo = h * HEAD_DIM + p * DIFF_QK
                y = yt[lo:lo + DIFF_QK, :]
                ms = jnp.mean(y * y, axis=0, keepdims=True)
                n = y * lax.rsqrt(ms + EPS) * g
                x1 = n[0:ROPE_HALF, :]
                x2 = n[ROPE_HALF:2 * ROPE_HALF, :]
                parts += [x1 * c - x2 * s, x2 * c + x1 * s, n[2 * ROPE_HALF:, :]]
            out_ref[h * HEAD_DIM:(h + 1) * HEAD_DIM, :] = jnp.concatenate(parts, axis=0).astype(BF16)

    @pl.when((j == 1) | (j == 3))
    def _():
        out_ref[...] = yt.astype(BF16)


def _proj_t(x, g1, wt, fqg, dqg, cos_t, sin_t):
    return pl.pallas_call(
        _proj_t_kernel,
        grid=(NT, 4),
        in_specs=[
            pl.BlockSpec((T, D_MODEL), lambda i, j: (i, 0)),
            pl.BlockSpec((1, D_MODEL), lambda i, j: (0, 0)),
            pl.BlockSpec((GROUP_W, D_MODEL), lambda i, j: (j, 0)),
            pl.BlockSpec((HEAD_DIM, 1), lambda i, j: (0, 0)),
            pl.BlockSpec((DIFF_QK, 1), lambda i, j: (0, 0)),
            pl.BlockSpec((ROPE_HALF, T), lambda i, j: (0, i)),
            pl.BlockSpec((ROPE_HALF, T), lambda i, j: (0, i)),
        ],
        out_specs=pl.BlockSpec((None, None, GROUP_W, T), lambda i, j: (j, i, 0, 0)),
        out_shape=jax.ShapeDtypeStruct((4, NT, GROUP_W, T), BF16),
        scratch_shapes=[pltpu.VMEM((T, D_MODEL), BF16)],
        compiler_params=pltpu.CompilerParams(
            dimension_semantics=("arbitrary", "arbitrary"), vmem_limit_bytes=VMEM_LIMIT),
        name="proj_t",
    )(x, g1, wt, fqg, dqg, cos_t, sin_t)


def _proj_n_kernel(x_ref, g1_ref, wk_ref, wfg_ref, bf_ref, fkg_ref, dkg_ref,
                   cosf_ref, sinn_ref, sinp_ref, k_ref, cneg_ref, h_ref, carry_ref):
    i = pl.program_id(0)
    j = pl.program_id(1)

    @pl.when(j == 0)
    def _():
        h = _rms_rows(x_ref[...], g1_ref[...]).astype(BF16)
        h_ref[...] = h
        z = jnp.dot(h, wfg_ref[...], preferred_element_type=F32) + bf_ref[...]
        c = jnp.minimum(z, 0.0) - jnp.log1p(jnp.exp(-jnp.abs(z)))
        row = lax.broadcasted_iota(jnp.int32, c.shape, 0)
        shift = 1
        while shift < T:
            c = c + jnp.where(row >= shift, pltpu.roll(c, shift, axis=0), 0.0)
            shift *= 2

        @pl.when(i == 0)
        def _():
            carry_ref[...] = jnp.zeros_like(carry_ref)

        c = c + carry_ref[...]
        carry_ref[...] = c[T - 1:T, :]
        lane = lax.broadcasted_iota(jnp.int32, c.shape, 1)
        for h_idx in range(HEADS):
            col = jnp.sum(jnp.where(lane == h_idx, c, 0.0), axis=-1, keepdims=True)
            cneg_ref[h_idx] = jnp.broadcast_to(col * (-LOG2E), (T, LANES))

    y = jnp.dot(h_ref[...], wk_ref[...], preferred_element_type=F32)

    @pl.when(j == 0)
    def _():
        g = fkg_ref[...]
        for h_idx in range(HEADS):
            k_ref[h_idx] = _rms_rows(y[:, h_idx * HEAD_DIM:(h_idx + 1) * HEAD_DIM], g).astype(BF16)

    @pl.when(j == 1)
    def _():
        g = dkg_ref[...]
        cf = cosf_ref[...]
        sn = sinn_ref[...]
        sp = sinp_ref[...]
        lo = lax.broadcasted_iota(jnp.int32, (T, HEAD_DIM), 1) < DIFF_QK
        for h_idx in range(HEADS):
            yh = y[:, h_idx * HEAD_DIM:(h_idx + 1) * HEAD_DIM]
            sq = yh * yh
            ms_lo = jnp.sum(jnp.where(lo, sq, 0.0), axis=-1, keepdims=True) * (1.0 / DIFF_QK)
            ms_hi = jnp.sum(jnp.where(lo, 0.0, sq), axis=-1, keepdims=True) * (1.0 / DIFF_QK)
            r = jnp.where(lo, lax.rsqrt(ms_lo + EPS), lax.rsqrt(ms_hi + EPS))
            n = yh * r * g
            rot = (n * cf + pltpu.roll(n, HEAD_DIM - ROPE_HALF, axis=1) * sn
                   + pltpu.roll(n, ROPE_HALF, axis=1) * sp)
            k_ref[h_idx] = rot.astype(BF16)


def _proj_n(x, g1, wk, wfg, bfg, fkg, dkg, cosf, sinn, sinp):
    row_tab = pl.BlockSpec((T, HEAD_DIM), lambda i, j: (i, 0))
    vec = lambda n: pl.BlockSpec((1, n), lambda i, j: (0, 0))
    return pl.pallas_call(
        _proj_n_kernel,
        grid=(NT, 2),
        in_specs=[
            pl.BlockSpec((T, D_MODEL), lambda i, j: (i, 0)),
            vec(D_MODEL),
            pl.BlockSpec((D_MODEL, GROUP_W), lambda i, j: (0, j)),
            pl.BlockSpec((D_MODEL, LANES), lambda i, j: (0, 0)),
            vec(LANES), vec(HEAD_DIM), vec(HEAD_DIM),
            row_tab, row_tab, row_tab,
        ],
        out_specs=[
            pl.BlockSpec((HEADS, T, HEAD_DIM), lambda i, j: (j, i, 0)),
            pl.BlockSpec((HEADS, T, LANES), lambda i, j: (0, i, 0)),
        ],
        out_shape=[
            jax.ShapeDtypeStruct((2 * HEADS, SEQ, HEAD_DIM), BF16),
            jax.ShapeDtypeStruct((HEADS, SEQ, LANES), F32),
        ],
        scratch_shapes=[pltpu.VMEM((T, D_MODEL), BF16), pltpu.VMEM((1, LANES), F32)],
        compiler_params=pltpu.CompilerParams(
            dimension_semantics=("arbitrary", "arbitrary"), vmem_limit_bytes=VMEM_LIMIT),
        name="proj_n",
    )(x, g1, wk, wfg, bfg, fkg, dkg, cosf, sinn, sinp)


def _online_softmax_step(s, vt, m_ref, l_ref, acc_ref):
    m_old = m_ref[...]
    m_new = jnp.maximum(m_old, jnp.max(s, axis=0, keepdims=True))
    p = jnp.exp2(s - m_new)
    alpha = jnp.exp2(m_old - m_new)
    l_ref[...] = alpha * l_ref[...] + jnp.sum(p, axis=0, keepdims=True)
    acc_ref[...] = alpha * acc_ref[...] + jnp.dot(vt, p.astype(BF16), preferred_element_type=F32)
    m_ref[...] = m_new


def _init_stats(m_ref, l_ref, acc_ref):
    m_ref[...] = jnp.full_like(m_ref, NEG)
    l_ref[...] = jnp.zeros_like(l_ref)
    acc_ref[...] = jnp.zeros_like(acc_ref)


def _fox_kernel(q_ref, k_ref, vt_ref, cneg_ref, o_ref, m_ref, l_ref, acc_ref):
    i = pl.program_id(1)
    q = q_ref[...]
    _init_stats(m_ref, l_ref, acc_ref)

    def tile(kv, masked):
        off = pl.multiple_of(kv * T, T)
        cn = cneg_ref[pl.ds(off, T), :]
        s = jnp.dot(k_ref[pl.ds(off, T), :], q, preferred_element_type=F32)
        s = s + jnp.concatenate([cn] * (T // LANES), axis=1)
        if masked:
            key = lax.broadcasted_iota(jnp.int32, (T, T), 0)
            qry = lax.broadcasted_iota(jnp.int32, (T, T), 1)
            s = jnp.where(key <= qry, s, NEG)
        _online_softmax_step(s, vt_ref[kv], m_ref, l_ref, acc_ref)

    def body(kv, carry):
        tile(kv, False)
        return carry

    lax.fori_loop(0, i, body, 0)
    tile(i, True)
    o = acc_ref[...] / l_ref[...]
    o_ref[...] = o.T.astype(BF16)


def _diff_kernel(q_ref, k_ref, vt_ref, lq1_ref, lk1_ref, lq2_ref, lk2_ref, sg_ref,
                 o_ref, m_ref, l_ref, acc_ref):
    i = pl.program_id(1)
    q = q_ref[...]
    part = lax.broadcasted_iota(jnp.int32, q.shape, 0) < DIFF_QK
    zero = jnp.zeros_like(q)
    qq = jnp.concatenate([jnp.where(part, q, zero), jnp.where(part, zero, q)], axis=1)
    _init_stats(m_ref, l_ref, acc_ref)

    def tile(kv, masked):
        off = pl.multiple_of(kv * T, T)
        s = jnp.dot(k_ref[pl.ds(off, T), :], qq, preferred_element_type=F32)
        if masked:
            key = lax.broadcasted_iota(jnp.int32, (T, 2 * T), 0)
            qry = lax.broadcasted_iota(jnp.int32, (T, 2 * T), 1) & (T - 1)
            s = jnp.where((key >> CHUNK_SHIFT) <= (qry >> CHUNK_SHIFT), s, NEG)
        _online_softmax_step(s, vt_ref[kv], m_ref, l_ref, acc_ref)

    def body(kv, carry):
        tile(kv, False)
        return carry

    lax.fori_loop(0, i, body, 0)
    tile(i, True)

    lam = (jnp.exp(jnp.sum(lq1_ref[...] * lk1_ref[...], axis=-1, keepdims=True))
           - jnp.exp(jnp.sum(lq2_ref[...] * lk2_ref[...], axis=-1, keepdims=True))
           + LAMBDA_INIT)
    o_all = acc_ref[...] / l_ref[...]
    o = o_all[:, :T] - lam * o_all[:, T:]
    ms = jnp.mean(o * o, axis=0, keepdims=True)
    o = o * lax.rsqrt(ms + EPS) * (sg_ref[...] * (1.0 - LAMBDA_INIT))
    o_ref[...] = o.T.astype(BF16)


def _attn_specs(q_seg, v_seg, k_off):
    return [
        pl.BlockSpec((None, None, HEAD_DIM, T), lambda h, i: (q_seg, i, h, 0)),
        pl.BlockSpec((None, SEQ, HEAD_DIM), lambda h, i: (k_off + h, 0, 0)),
        pl.BlockSpec((None, NT, HEAD_DIM, T), lambda h, i: (v_seg, 0, h, 0)),
    ]


def _fox_attn(pt, k_all, cneg):
    return pl.pallas_call(
        _fox_kernel,
        grid=(HEADS, NT),
        in_specs=_attn_specs(0, 1, 0) + [pl.BlockSpec((None, SEQ, LANES), lambda h, i: (h, 0, 0))],
        out_specs=pl.BlockSpec((T, HEAD_DIM), lambda h, i: (i, h)),
        out_shape=jax.ShapeDtypeStruct((SEQ, GROUP_W), BF16),
        scratch_shapes=[pltpu.VMEM((1, T), F32), pltpu.VMEM((1, T), F32), pltpu.VMEM((HEAD_DIM, T), F32)],
        compiler_params=pltpu.CompilerParams(
            dimension_semantics=("arbitrary", "arbitrary"), vmem_limit_bytes=VMEM_LIMIT),
        name="fox_attn",
    )(pt, k_all, pt, cneg)


def _diff_attn(pt, k_all, lq1, lk1, lq2, lk2, sg):
    lam_vec = pl.BlockSpec((1, DIFF_QK), lambda h, i: (0, 0))
    return pl.pallas_call(
        _diff_kernel,
        grid=(HEADS, NT),
        in_specs=_attn_specs(2, 3, HEADS) + [lam_vec] * 4 + [pl.BlockSpec((HEAD_DIM, 1), lambda h, i: (0, 0))],
        out_specs=pl.BlockSpec((T, HEAD_DIM), lambda h, i: (i, h)),
        out_shape=jax.ShapeDtypeStruct((SEQ, GROUP_W), BF16),
        scratch_shapes=[pltpu.VMEM((1, 2 * T), F32), pltpu.VMEM((1, 2 * T), F32),
                        pltpu.VMEM((HEAD_DIM, 2 * T), F32)],
        compiler_params=pltpu.CompilerParams(
            dimension_semantics=("arbitrary", "arbitrary"), vmem_limit_bytes=VMEM_LIMIT),
        name="diff_attn",
    )(pt, k_all, pt, lq1, lk1, lq2, lk2, sg)


def _outproj_kernel(of_ref, od_ref, wo_ref, x_ref, g2_ref, x1_ref, h2_ref):
    mix = (jnp.dot(of_ref[...], wo_ref[0:GROUP_W, :], preferred_element_type=F32)
           + jnp.dot(od_ref[...], wo_ref[GROUP_W:, :], preferred_element_type=F32))
    x1 = x_ref[...] + mix
    x1_ref[...] = x1
    h2_ref[...] = _rms_rows(x1, g2_ref[...]).astype(BF16)


def _outproj(o_f, o_d, wo, x, g2):
    return pl.pallas_call(
        _outproj_kernel,
        grid=(NT,),
        in_specs=[
            pl.BlockSpec((T, GROUP_W), lambda i: (i, 0)),
            pl.BlockSpec((T, GROUP_W), lambda i: (i, 0)),
            pl.BlockSpec((2 * GROUP_W, D_MODEL), lambda i: (0, 0)),
            pl.BlockSpec((T, D_MODEL), lambda i: (i, 0)),
            pl.BlockSpec((1, D_MODEL), lambda i: (0, 0)),
        ],
        out_specs=[pl.BlockSpec((T, D_MODEL), lambda i: (i, 0)),
                   pl.BlockSpec((T, D_MODEL), lambda i: (i, 0))],
        out_shape=[jax.ShapeDtypeStruct((SEQ, D_MODEL), F32),
                   jax.ShapeDtypeStruct((SEQ, D_MODEL), BF16)],
        compiler_params=pltpu.CompilerParams(
            dimension_semantics=("arbitrary",), vmem_limit_bytes=VMEM_LIMIT),
        name="outproj",
    )(o_f, o_d, wo, x, g2)


def _ffn_kernel(h2_ref, halo_ref, wg_ref, wv_ref, cwg_ref, cwv_ref, cbg_ref, cbv_ref, wd_ref, x1_ref,
                out_ref, he_ref, acc_ref):
    i = pl.program_id(0)
    j = pl.program_id(1)

    @pl.when(j == 0)
    def _():
        @pl.when(i == 0)
        def _():
            he_ref[0:HALO, :] = jnp.zeros((HALO, D_MODEL), BF16)

        @pl.when(i > 0)
        def _():
            he_ref[0:HALO, :] = halo_ref[...]

        he_ref[HALO:, :] = h2_ref[...]
        acc_ref[...] = jnp.zeros_like(acc_ref)

    he = he_ref[...]

    def conv(w_ref, cw_ref, cb_ref):
        u = jnp.dot(he, w_ref[...], preferred_element_type=F32)
        cw = cw_ref[...]
        u_t = u[HALO:, :]
        u_t1 = pltpu.roll(u, 1, axis=0)[HALO:, :]
        u_t2 = pltpu.roll(u, 2, axis=0)[HALO:, :]
        return cw[2:3, :] * u_t + cw[1:2, :] * u_t1 + cw[0:1, :] * u_t2 + cb_ref[...]

    gate = conv(wg_ref, cwg_ref, cbg_ref)
    val = conv(wv_ref, cwv_ref, cbv_ref)
    act = (gate * jax.nn.sigmoid(gate) * val).astype(BF16)
    acc_ref[...] += jnp.dot(act, wd_ref[...], preferred_element_type=F32)

    @pl.when(j == NF - 1)
    def _():
        out_ref[...] = x1_ref[...] + acc_ref[...]


def _ffn(h2, wg, wv, cwg, cwv, cbg, cbv, wd, x1):
    halo_blocks_per_tile = T // HALO
    return pl.pallas_call(
        _ffn_kernel,
        grid=(NT, NF),
        in_specs=[
            pl.BlockSpec((T, D_MODEL), lambda i, j: (i, 0)),
            pl.BlockSpec((HALO, D_MODEL), lambda i, j: (jnp.maximum(i * halo_blocks_per_tile - 1, 0), 0)),
            pl.BlockSpec((D_MODEL, TF), lambda i, j: (0, j)),
            pl.BlockSpec((D_MODEL, TF), lambda i, j: (0, j)),
            pl.BlockSpec((3, TF), lambda i, j: (0, j)),
            pl.BlockSpec((3, TF), lambda i, j: (0, j)),
            pl.BlockSpec((1, TF), lambda i, j: (0, j)),
            pl.BlockSpec((1, TF), lambda i, j: (0, j)),
            pl.BlockSpec((TF, D_MODEL), lambda i, j: (j, 0)),
            pl.BlockSpec((T, D_MODEL), lambda i, j: (i, 0)),
        ],
        out_specs=pl.BlockSpec((T, D_MODEL), lambda i, j: (i, 0)),
        out_shape=jax.ShapeDtypeStruct((SEQ, D_MODEL), F32),
        scratch_shapes=[pltpu.VMEM((HALO + T, D_MODEL), BF16), pltpu.VMEM((T, D_MODEL), F32)],
        compiler_params=pltpu.CompilerParams(
            dimension_semantics=("arbitrary", "arbitrary"), vmem_limit_bytes=VMEM_LIMIT),
        name="ffn",
    )(h2, h2, wg, wv, cwg, cwv, cbg, cbv, wd, x1)


def _rope_tables():
    inv_freq = ROPE_THETA ** (-jnp.arange(0, 2 * ROPE_HALF, 2, dtype=F32) / (2 * ROPE_HALF))
    ang = jnp.arange(SEQ, dtype=F32)[:, None] * inv_freq[None, :]
    cos, sin = jnp.cos(ang), jnp.sin(ang)
    ones = jnp.ones((SEQ, DIFF_QK - 2 * ROPE_HALF), F32)
    zeros = jnp.zeros((SEQ, DIFF_QK - 2 * ROPE_HALF), F32)
    z8 = jnp.zeros((SEQ, ROPE_HALF), F32)
    cosf = jnp.tile(jnp.concatenate([cos, cos, ones], axis=1), (1, 2))
    sinn = jnp.tile(jnp.concatenate([-sin, z8, zeros], axis=1), (1, 2))
    sinp = jnp.tile(jnp.concatenate([z8, sin, zeros], axis=1), (1, 2))
    return cos.T, sin.T, cosf, sinn, sinp


def kernel(x, norm1_g, w_in, b_forget, fox_q_g, fox_k_g, diff_q_g, diff_k_g, lam_q1, lam_k1, lam_q2,
           lam_k2, diff_subln_g, w_out, norm2_g, w_up, conv_w, conv_b, w_down):
    assert x.shape == (1, SEQ, D_MODEL) and w_in.shape[0] == 1
    xs = x[0]
    w = w_in[0]
    c0 = 3 * GROUP_W + HEADS
    seg = lambda a: w[:, a:a + GROUP_W]
    wt = jnp.concatenate([seg(0), seg(2 * GROUP_W), seg(c0), seg(c0 + 2 * GROUP_W)], axis=1)
    wt = wt.astype(BF16).T
    wk = jnp.concatenate([seg(GROUP_W), seg(c0 + GROUP_W)], axis=1).astype(BF16)
    wfg = jnp.pad(w[:, 3 * GROUP_W:c0], ((0, 0), (0, LANES - HEADS))).astype(BF16)
    bfg = jnp.pad(b_forget, ((0, 0), (0, LANES - HEADS)))
    cos_t, sin_t, cosf, sinn, sinp = _rope_tables()

    pt = _proj_t(xs, norm1_g, wt, fox_q_g[0][:, None], diff_q_g[0][:, None], cos_t, sin_t)
    k_all, cneg = _proj_n(xs, norm1_g, wk, wfg, bfg, fox_k_g, jnp.tile(diff_k_g, (1, 2)),
                          cosf, sinn, sinp)
    o_f = _fox_attn(pt, k_all, cneg)
    o_d = _diff_attn(pt, k_all, lam_q1, lam_k1, lam_q2, lam_k2, diff_subln_g[0][:, None])
    x1, h2 = _outproj(o_f, o_d, w_out[0].astype(BF16), xs, norm2_g)

    ff_pad = ((0, 0), (0, FF_PAD - D_FF))
    wu = w_up[0]
    wg = jnp.pad(wu[:, :D_FF], ff_pad).astype(BF16)
    wv = jnp.pad(wu[:, D_FF:], ff_pad).astype(BF16)
    cwg = jnp.pad(conv_w[0][:, :D_FF], ff_pad)
    cwv = jnp.pad(conv_w[0][:, D_FF:], ff_pad)
    cbg = jnp.pad(conv_b[:, :D_FF], ff_pad)
    cbv = jnp.pad(conv_b[:, D_FF:], ff_pad)
    wd = jnp.pad(w_down[0], ((0, FF_PAD - D_FF), (0, 0))).astype(BF16)
    out = _ffn(h2, wg, wv, cwg, cwv, cbg, cbv, wd, x1)
    return out[None]
```

```python
import functools
import math

import jax
import jax.numpy as jnp
from jax import lax
from jax.experimental import pallas as pl
from jax.experimental.pallas import tpu as pltpu

D_MODEL = 2048
SEQ = 8192
HEADS = 8
HEAD_DIM = 128
DIFF_QK = 64
ROPE_HALF = 8
ROPE_THETA = 500000.0
GROUP_W = HEADS * HEAD_DIM
D_FF = 5504
EPS = 1e-6
NEG = -1e30
LOG2E = 1.4426950408889634
LAMBDA_INIT = 0.8 - 0.6 * math.exp(-0.3 * 0)
CHUNK_SHIFT = 6

LANES = 128
T = 512
NT = SEQ // T
FF_PAD = 5632
TF = 512
NF = FF_PAD // TF
HALO = 16
VMEM_LIMIT = 56 * 1024 * 1024

F32 = jnp.float32
BF16 = jnp.bfloat16


def _rms_rows(x, g):
    ms = jnp.mean(x * x, axis=-1, keepdims=True)
    return x * lax.rsqrt(ms + EPS) * g


def _proj_t_kernel(x_ref, g1_ref, wt_ref, fqg_ref, dqg_ref, cos_ref, sin_ref, out_ref, h_ref):
    j = pl.program_id(1)

    @pl.when(j == 0)
    def _():
        h_ref[...] = _rms_rows(x_ref[...], g1_ref[...]).astype(BF16)

    yt = lax.dot_general(wt_ref[...], h_ref[...], (((1,), (1,)), ((), ())),
                         preferred_element_type=F32)

    @pl.when(j == 0)
    def _():
        g = fqg_ref[...] * (HEAD_DIM ** -0.5 * LOG2E)
        for h in range(HEADS):
            y = yt[h * HEAD_DIM:(h + 1) * HEAD_DIM, :]
            ms = jnp.mean(y * y, axis=0, keepdims=True)
            out_ref[h * HEAD_DIM:(h + 1) * HEAD_DIM, :] = (y * lax.rsqrt(ms + EPS) * g).astype(BF16)

    @pl.when(j == 2)
    def _():
        g = dqg_ref[...] * (DIFF_QK ** -0.5 * LOG2E)
        c = cos_ref[...]
        s = sin_ref[...]
        for h in range(HEADS):
            parts = []
            for p in range(2):
                lo = h * HEAD_DIM + p * DIFF_QK
                y = yt[lo:lo + DIFF_QK, :]
                ms = jnp.mean(y * y, axis=0, keepdims=True)
                n = y * lax.rsqrt(ms + EPS) * g
                x1 = n[0:ROPE_HALF, :]
                x2 = n[ROPE_HALF:2 * ROPE_HALF, :]
                parts += [x1 * c - x2 * s, x2 * c + x1 * s, n[2 * ROPE_HALF:, :]]
            out_ref[h * HEAD_DIM:(h + 1) * HEAD_DIM, :] = jnp.concatenate(parts, axis=0).astype(BF16)

    @pl.when((j == 1) | (j == 3))
    def _():
        out_ref[...] = yt.astype(BF16)


def _proj_t(x, g1, wt, fqg, dqg, cos_t, sin_t):
    return pl.pallas_call(
        _proj_t_kernel,
        grid=(NT, 4),
        in_specs=[
            pl.BlockSpec((T, D_MODEL), lambda i, j: (i, 0)),
            pl.BlockSpec((1, D_MODEL), lambda i, j: (0, 0)),
            pl.BlockSpec((GROUP_W, D_MODEL), lambda i, j: (j, 0)),
            pl.BlockSpec((HEAD_DIM, 1), lambda i, j: (0, 0)),
            pl.BlockSpec((DIFF_QK, 1), lambda i, j: (0, 0)),
            pl.BlockSpec((ROPE_HALF, T), lambda i, j: (0, i)),
            pl.BlockSpec((ROPE_HALF, T), lambda i, j: (0, i)),
        ],
        out_specs=pl.BlockSpec((None, None, GROUP_W, T), lambda i, j: (j, i, 0, 0)),
        out_shape=jax.ShapeDtypeStruct((4, NT, GROUP_W, T), BF16),
        scratch_shapes=[pltpu.VMEM((T, D_MODEL), BF16)],
        compiler_params=pltpu.CompilerParams(
            dimension_semantics=("arbitrary", "arbitrary"), vmem_limit_bytes=VMEM_LIMIT),
        name="proj_t",
    )(x, g1, wt, fqg, dqg, cos_t, sin_t)


def _proj_n_kernel(x_ref, g1_ref, wk_ref, wfg_ref, bf_ref, fkg_ref, dkg_ref,
                   cosf_ref, sinn_ref, sinp_ref, k_ref, cneg_ref, h_ref, carry_ref):
    i = pl.program_id(0)
    j = pl.program_id(1)

    @pl.when(j == 0)
    def _():
        h = _rms_rows(x_ref[...], g1_ref[...]).astype(BF16)
        h_ref[...] = h
        z = jnp.dot(h, wfg_ref[...], preferred_element_type=F32) + bf_ref[...]
        c = jnp.minimum(z, 0.0) - jnp.log1p(jnp.exp(-jnp.abs(z)))
        row = lax.broadcasted_iota(jnp.int32, c.shape, 0)
        shift = 1
        while shift < T:
            c = c + jnp.where(row >= shift, pltpu.roll(c, shift, axis=0), 0.0)
            shift *= 2

        @pl.when(i == 0)
        def _():
            carry_ref[...] = jnp.zeros_like(carry_ref)

        c = c + carry_ref[...]
        carry_ref[...] = c[T - 1:T, :]
        lane = lax.broadcasted_iota(jnp.int32, c.shape, 1)
        for h_idx in range(HEADS):
            col = jnp.sum(jnp.where(lane == h_idx, c, 0.0), axis=-1, keepdims=True)
            cneg_ref[h_idx] = jnp.broadcast_to(col * (-LOG2E), (T, LANES))

    y = jnp.dot(h_ref[...], wk_ref[...], preferred_element_type=F32)

    @pl.when(j == 0)
    def _():
        g = fkg_ref[...]
        for h_idx in range(HEADS):
            k_ref[h_idx] = _rms_rows(y[:, h_idx * HEAD_DIM:(h_idx + 1) * HEAD_DIM], g).astype(BF16)

    @pl.when(j == 1)
    def _():
        g = dkg_ref[...]
        cf = cosf_ref[...]
        sn = sinn_ref[...]
        sp = sinp_ref[...]
        lo = lax.broadcasted_iota(jnp.int32, (T, HEAD_DIM), 1) < DIFF_QK
        for h_idx in range(HEADS):
            yh = y[:, h_idx * HEAD_DIM:(h_idx + 1) * HEAD_DIM]
            sq = yh * yh
            ms_lo = jnp.sum(jnp.where(lo, sq, 0.0), axis=-1, keepdims=True) * (1.0 / DIFF_QK)
            ms_hi = jnp.sum(jnp.where(lo, 0.0, sq), axis=-1, keepdims=True) * (1.0 / DIFF_QK)
            r = jnp.where(lo, lax.rsqrt(ms_lo + EPS), lax.rsqrt(ms_hi + EPS))
            n = yh * r * g
            rot = (n * cf + pltpu.roll(n, HEAD_DIM - ROPE_HALF, axis=1) * sn
                   + pltpu.roll(n, ROPE_HALF, axis=1) * sp)
            k_ref[h_idx] = rot.astype(BF16)


def _proj_n(x, g1, wk, wfg, bfg, fkg, dkg, cosf, sinn, sinp):
    row_tab = pl.BlockSpec((T, HEAD_DIM), lambda i, j: (i, 0))
    vec = lambda n: pl.BlockSpec((1, n), lambda i, j: (0, 0))
    return pl.pallas_call(
        _proj_n_kernel,
        grid=(NT, 2),
        in_specs=[
            pl.BlockSpec((T, D_MODEL), lambda i, j: (i, 0)),
            vec(D_MODEL),
            pl.BlockSpec((D_MODEL, GROUP_W), lambda i, j: (0, j)),
            pl.BlockSpec((D_MODEL, LANES), lambda i, j: (0, 0)),
            vec(LANES), vec(HEAD_DIM), vec(HEAD_DIM),
            row_tab, row_tab, row_tab,
        ],
        out_specs=[
            pl.BlockSpec((HEADS, T, HEAD_DIM), lambda i, j: (j, i, 0)),
            pl.BlockSpec((HEADS, T, LANES), lambda i, j: (0, i, 0)),
        ],
        out_shape=[
            jax.ShapeDtypeStruct((2 * HEADS, SEQ, HEAD_DIM), BF16),
            jax.ShapeDtypeStruct((HEADS, SEQ, LANES), F32),
        ],
        scratch_shapes=[pltpu.VMEM((T, D_MODEL), BF16), pltpu.VMEM((1, LANES), F32)],
        compiler_params=pltpu.CompilerParams(
            dimension_semantics=("arbitrary", "arbitrary"), vmem_limit_bytes=VMEM_LIMIT),
        name="proj_n",
    )(x, g1, wk, wfg, bfg, fkg, dkg, cosf, sinn, sinp)


def _online_softmax_step(s, vt, m_ref, l_ref, acc_ref):
    m_old = m_ref[...]
    m_new = jnp.maximum(m_old, jnp.max(s, axis=0, keepdims=True))
    p = jnp.exp2(s - m_new)
    alpha = jnp.exp2(m_old - m_new)
    l_ref[...] = alpha * l_ref[...] + jnp.sum(p, axis=0, keepdims=True)
    acc_ref[...] = alpha * acc_ref[...] + jnp.dot(vt, p.astype(BF16), preferred_element_type=F32)
    m_ref[...] = m_new


def _init_stats(m_ref, l_ref, acc_ref):
    m_ref[...] = jnp.full_like(m_ref, NEG)
    l_ref[...] = jnp.zeros_like(l_ref)
    acc_ref[...] = jnp.zeros_like(acc_ref)


def _attend(i, scores, mask, vt_ref, sa_ref, sb_ref, m_ref, l_ref, acc_ref):
    _init_stats(m_ref, l_ref, acc_ref)

    def consume(s, kv):
        _online_softmax_step(s, vt_ref[kv], m_ref, l_ref, acc_ref)

    sa_ref[...] = scores(0)

    def body(a, carry):
        sb_ref[...] = scores(2 * a + 1)
        consume(sa_ref[...], 2 * a)
        sa_ref[...] = scores(2 * a + 2)
        consume(sb_ref[...], 2 * a + 1)
        return carry

    lax.fori_loop(0, i // 2, body, 0)
    odd = (i % 2) == 1

    @pl.when(odd)
    def _():
        sb_ref[...] = scores(i)
        consume(sa_ref[...], i - 1)
        consume(mask(sb_ref[...]), i)

    @pl.when(jnp.logical_not(odd))
    def _():
        consume(mask(sa_ref[...]), i)


def _fox_kernel(q_ref, k_ref, vt_ref, cneg_ref, o_ref, sa_ref, sb_ref, m_ref, l_ref, acc_ref):
    i = pl.program_id(1)
    q = q_ref[...]

    def scores(kv):
        off = pl.multiple_of(kv * T, T)
        cn = cneg_ref[pl.ds(off, T), :]
        s = jnp.dot(k_ref[pl.ds(off, T), :], q, preferred_element_type=F32)
        return s + jnp.concatenate([cn] * (T // LANES), axis=1)

    def mask(s):
        key = lax.broadcasted_iota(jnp.int32, (T, T), 0)
        qry = lax.broadcasted_iota(jnp.int32, (T, T), 1)
        return jnp.where(key <= qry, s, NEG)

    _attend(i, scores, mask, vt_ref, sa_ref, sb_ref, m_ref, l_ref, acc_ref)
    o = acc_ref[...] / l_ref[...]
    o_ref[...] = o.T.astype(BF16)


def _diff_kernel(q_ref, k_ref, vt_ref, lq1_ref, lk1_ref, lq2_ref, lk2_ref, sg_ref,
                 o_ref, sa_ref, sb_ref, m_ref, l_ref, acc_ref):
    i = pl.program_id(1)
    q = q_ref[...]
    part = lax.broadcasted_iota(jnp.int32, q.shape, 0) < DIFF_QK
    zero = jnp.zeros_like(q)
    qq = jnp.concatenate([jnp.where(part, q, zero), jnp.where(part, zero, q)], axis=1)

    def scores(kv):
        off = pl.multiple_of(kv * T, T)
        return jnp.dot(k_ref[pl.ds(off, T), :], qq, preferred_element_type=F32)

    def mask(s):
        key = lax.broadcasted_iota(jnp.int32, (T, 2 * T), 0)
        qry = lax.broadcasted_iota(jnp.int32, (T, 2 * T), 1) & (T - 1)
        return jnp.where((key >> CHUNK_SHIFT) <= (qry >> CHUNK_SHIFT), s, NEG)

    _attend(i, scores, mask, vt_ref, sa_ref, sb_ref, m_ref, l_ref, acc_ref)

    lam = (jnp.exp(jnp.sum(lq1_ref[...] * lk1_ref[...], axis=-1, keepdims=True))
           - jnp.exp(jnp.sum(lq2_ref[...] * lk2_ref[...], axis=-1, keepdims=True))
           + LAMBDA_INIT)
    o_all = acc_ref[...] / l_ref[...]
    o = o_all[:, :T] - lam * o_all[:, T:]
    ms = jnp.mean(o * o, axis=0, keepdims=True)
    o = o * lax.rsqrt(ms + EPS) * (sg_ref[...] * (1.0 - LAMBDA_INIT))
    o_ref[...] = o.T.astype(BF16)


def _attn_specs(q_seg, v_seg, k_off):
    return [
        pl.BlockSpec((None, None, HEAD_DIM, T), lambda h, i: (q_seg, i, h, 0)),
        pl.BlockSpec((None, SEQ, HEAD_DIM), lambda h, i: (k_off + h, 0, 0)),
        pl.BlockSpec((None, NT, HEAD_DIM, T), lambda h, i: (v_seg, 0, h, 0)),
    ]


def _attn_scratch(width):
    return [pltpu.VMEM((T, width), F32), pltpu.VMEM((T, width), F32),
            pltpu.VMEM((1, width), F32), pltpu.VMEM((1, width), F32), pltpu.VMEM((HEAD_DIM, width), F32)]


def _fox_attn(pt, k_all, cneg):
    return pl.pallas_call(
        _fox_kernel,
        grid=(HEADS, NT),
        in_specs=_attn_specs(0, 1, 0) + [pl.BlockSpec((None, SEQ, LANES), lambda h, i: (h, 0, 0))],
        out_specs=pl.BlockSpec((T, HEAD_DIM), lambda h, i: (i, h)),
        out_shape=jax.ShapeDtypeStruct((SEQ, GROUP_W), BF16),
        scratch_shapes=_attn_scratch(T),
        compiler_params=pltpu.CompilerParams(
            dimension_semantics=("arbitrary", "arbitrary"), vmem_limit_bytes=VMEM_LIMIT),
        name="fox_attn",
    )(pt, k_all, pt, cneg)


def _diff_attn(pt, k_all, lq1, lk1, lq2, lk2, sg):
    lam_vec = pl.BlockSpec((1, DIFF_QK), lambda h, i: (0, 0))
    return pl.pallas_call(
        _diff_kernel,
        grid=(HEADS, NT),
        in_specs=_attn_specs(2, 3, HEADS) + [lam_vec] * 4 + [pl.BlockSpec((HEAD_DIM, 1), lambda h, i: (0, 0))],
        out_specs=pl.BlockSpec((T, HEAD_DIM), lambda h, i: (i, h)),
        out_shape=jax.ShapeDtypeStruct((SEQ, GROUP_W), BF16),
        scratch_shapes=_attn_scratch(2 * T),
        compiler_params=pltpu.CompilerParams(
            dimension_semantics=("arbitrary", "arbitrary"), vmem_limit_bytes=VMEM_LIMIT),
        name="diff_attn",
    )(pt, k_all, pt, lq1, lk1, lq2, lk2, sg)


def _outproj_kernel(of_ref, od_ref, wo_ref, x_ref, g2_ref, x1_ref, h2_ref):
    mix = (jnp.dot(of_ref[...], wo_ref[0:GROUP_W, :], preferred_element_type=F32)
           + jnp.dot(od_ref[...], wo_ref[GROUP_W:, :], preferred_element_type=F32))
    x1 = x_ref[...] + mix
    x1_ref[...] = x1
    h2_ref[...] = _rms_rows(x1, g2_ref[...]).astype(BF16)


def _outproj(o_f, o_d, wo, x, g2):
    return pl.pallas_call(
        _outproj_kernel,
        grid=(NT,),
        in_specs=[
            pl.BlockSpec((T, GROUP_W), lambda i: (i, 0)),
            pl.BlockSpec((T, GROUP_W), lambda i: (i, 0)),
            pl.BlockSpec((2 * GROUP_W, D_MODEL), lambda i: (0, 0)),
            pl.BlockSpec((T, D_MODEL), lambda i: (i, 0)),
            pl.BlockSpec((1, D_MODEL), lambda i: (0, 0)),
        ],
        out_specs=[pl.BlockSpec((T, D_MODEL), lambda i: (i, 0)),
                   pl.BlockSpec((T, D_MODEL), lambda i: (i, 0))],
        out_shape=[jax.ShapeDtypeStruct((SEQ, D_MODEL), F32),
                   jax.ShapeDtypeStruct((SEQ, D_MODEL), BF16)],
        compiler_params=pltpu.CompilerParams(
            dimension_semantics=("arbitrary",), vmem_limit_bytes=VMEM_LIMIT),
        name="outproj",
    )(o_f, o_d, wo, x, g2)


def _ffn_kernel(h2_ref, halo_ref, wg_ref, wv_ref, cwg_ref, cwv_ref, cbg_ref, cbv_ref, wd_ref, x1_ref,
                out_ref, he_ref, acc_ref):
    i = pl.program_id(0)
    j = pl.program_id(1)

    @pl.when(j == 0)
    def _():
        @pl.when(i == 0)
        def _():
            he_ref[0:HALO, :] = jnp.zeros((HALO, D_MODEL), BF16)

        @pl.when(i > 0)
        def _():
            he_ref[0:HALO, :] = halo_ref[...]

        he_ref[HALO:, :] = h2_ref[...]
        acc_ref[...] = jnp.zeros_like(acc_ref)

    he = he_ref[...]

    def conv(w_ref, cw_ref, cb_ref):
        u = jnp.dot(he, w_ref[...], preferred_element_type=F32)
        cw = cw_ref[...]
        u_t = u[HALO:, :]
        u_t1 = pltpu.roll(u, 1, axis=0)[HALO:, :]
        u_t2 = pltpu.roll(u, 2, axis=0)[HALO:, :]
        return cw[2:3, :] * u_t + cw[1:2, :] * u_t1 + cw[0:1, :] * u_t2 + cb_ref[...]

    gate = conv(wg_ref, cwg_ref, cbg_ref)
    val = conv(wv_ref, cwv_ref, cbv_ref)
    act = (gate * jax.nn.sigmoid(gate) * val).astype(BF16)
    acc_ref[...] += jnp.dot(act, wd_ref[...], preferred_element_type=F32)

    @pl.when(j == NF - 1)
    def _():
        out_ref[...] = x1_ref[...] + acc_ref[...]


def _ffn(h2, wg, wv, cwg, cwv, cbg, cbv, wd, x1):
    halo_blocks_per_tile = T // HALO
    return pl.pallas_call(
        _ffn_kernel,
        grid=(NT, NF),
        in_specs=[
            pl.BlockSpec((T, D_MODEL), lambda i, j: (i, 0)),
            pl.BlockSpec((HALO, D_MODEL), lambda i, j: (jnp.maximum(i * halo_blocks_per_tile - 1, 0), 0)),
            pl.BlockSpec((D_MODEL, TF), lambda i, j: (0, j)),
            pl.BlockSpec((D_MODEL, TF), lambda i, j: (0, j)),
            pl.BlockSpec((3, TF), lambda i, j: (0, j)),
            pl.BlockSpec((3, TF), lambda i, j: (0, j)),
            pl.BlockSpec((1, TF), lambda i, j: (0, j)),
            pl.BlockSpec((1, TF), lambda i, j: (0, j)),
            pl.BlockSpec((TF, D_MODEL), lambda i, j: (j, 0)),
            pl.BlockSpec((T, D_MODEL), lambda i, j: (i, 0)),
        ],
        out_specs=pl.BlockSpec((T, D_MODEL), lambda i, j: (i, 0)),
        out_shape=jax.ShapeDtypeStruct((SEQ, D_MODEL), F32),
        scratch_shapes=[pltpu.VMEM((HALO + T, D_MODEL), BF16), pltpu.VMEM((T, D_MODEL), F32)],
        compiler_params=pltpu.CompilerParams(
            dimension_semantics=("arbitrary", "arbitrary"), vmem_limit_bytes=VMEM_LIMIT),
        name="ffn",
    )(h2, h2, wg, wv, cwg, cwv, cbg, cbv, wd, x1)


def _rope_tables():
    inv_freq = ROPE_THETA ** (-jnp.arange(0, 2 * ROPE_HALF, 2, dtype=F32) / (2 * ROPE_HALF))
    ang = jnp.arange(SEQ, dtype=F32)[:, None] * inv_freq[None, :]
    cos, sin = jnp.cos(ang), jnp.sin(ang)
    ones = jnp.ones((SEQ, DIFF_QK - 2 * ROPE_HALF), F32)
    zeros = jnp.zeros((SEQ, DIFF_QK - 2 * ROPE_HALF), F32)
    z8 = jnp.zeros((SEQ, ROPE_HALF), F32)
    cosf = jnp.tile(jnp.concatenate([cos, cos, ones], axis=1), (1, 2))
    sinn = jnp.tile(jnp.concatenate([-sin, z8, zeros], axis=1), (1, 2))
    sinp = jnp.tile(jnp.concatenate([z8, sin, zeros], axis=1), (1, 2))
    return cos.T, sin.T, cosf, sinn, sinp


def kernel(x, norm1_g, w_in, b_forget, fox_q_g, fox_k_g, diff_q_g, diff_k_g, lam_q1, lam_k1, lam_q2,
           lam_k2, diff_subln_g, w_out, norm2_g, w_up, conv_w, conv_b, w_down):
    assert x.shape == (1, SEQ, D_MODEL) and w_in.shape[0] == 1
    xs = x[0]
    w = w_in[0]
    c0 = 3 * GROUP_W + HEADS
    seg = lambda a: w[:, a:a + GROUP_W]
    wt = jnp.concatenate([seg(0), seg(2 * GROUP_W), seg(c0), seg(c0 + 2 * GROUP_W)], axis=1)
    wt = wt.astype(BF16).T
    wk = jnp.concatenate([seg(GROUP_W), seg(c0 + GROUP_W)], axis=1).astype(BF16)
    wfg = jnp.pad(w[:, 3 * GROUP_W:c0], ((0, 0), (0, LANES - HEADS))).astype(BF16)
    bfg = jnp.pad(b_forget, ((0, 0), (0, LANES - HEADS)))
    cos_t, sin_t, cosf, sinn, sinp = _rope_tables()

    pt = _proj_t(xs, norm1_g, wt, fox_q_g[0][:, None], diff_q_g[0][:, None], cos_t, sin_t)
    k_all, cneg = _proj_n(xs, norm1_g, wk, wfg, bfg, fox_k_g, jnp.tile(diff_k_g, (1, 2)),
                          cosf, sinn, sinp)
    o_f = _fox_attn(pt, k_all, cneg)
    o_d = _diff_attn(pt, k_all, lam_q1, lam_k1, lam_q2, lam_k2, diff_subln_g[0][:, None])
    x1, h2 = _outproj(o_f, o_d, w_out[0].astype(BF16), xs, norm2_g)

    ff_pad = ((0, 0), (0, FF_PAD - D_FF))
    wu = w_up[0]
    wg = jnp.pad(wu[:, :D_FF], ff_pad).astype(BF16)
    wv = jnp.pad(wu[:, D_FF:], ff_pad).astype(BF16)
    cwg = jnp.pad(conv_w[0][:, :D_FF], ff_pad)
    cwv = jnp.pad(conv_w[0][:, D_FF:], ff_pad)
    cbg = jnp.pad(conv_b[:, :D_FF], ff_pad)
    cbv = jnp.pad(conv_b[:, D_FF:], ff_pad)
    wd = jnp.pad(w_down[0], ((0, FF_PAD - D_FF), (0, 0))).astype(BF16)
    out = _ffn(h2, wg, wv, cwg, cwv, cbg, cbv, wd, x1)
    return out[None]
```

```python
import functools
import math

import jax
import jax.numpy as jnp
from jax import lax
from jax.experimental import pallas as pl
from jax.experimental.pallas import tpu as pltpu

D_MODEL = 2048
SEQ = 8192
HEADS = 8
HEAD_DIM = 128
DIFF_QK = 64
ROPE_HALF = 8
ROPE_THETA = 500000.0
GROUP_W = HEADS * HEAD_DIM
D_FF = 5504
EPS = 1e-6
NEG = -1e30
LOG2E = 1.4426950408889634
LAMBDA_INIT = 0.8 - 0.6 * math.exp(-0.3 * 0)
CHUNK_SHIFT = 6

LANES = 128
T = 512
NT = SEQ // T
FF_PAD = 5632
TF = 512
NF = FF_PAD // TF
TQ = 2 * T
NQ = SEQ // TQ
BIAS_TERMS = 3
HALO = 16
VMEM_LIMIT = 56 * 1024 * 1024

F32 = jnp.float32
BF16 = jnp.bfloat16


def _rms_rows(x, g):
    ms = jnp.mean(x * x, axis=-1, keepdims=True)
    return x * lax.rsqrt(ms + EPS) * g


def _proj_t_kernel(x_ref, g1_ref, wt_ref, fqg_ref, dqg_ref, cos_ref, sin_ref, out_ref, h_ref):
    j = pl.program_id(1)

    @pl.when(j == 0)
    def _():
        h_ref[...] = _rms_rows(x_ref[...], g1_ref[...]).astype(BF16)

    yt = lax.dot_general(wt_ref[...], h_ref[...], (((1,), (1,)), ((), ())),
                         preferred_element_type=F32)

    @pl.when(j == 0)
    def _():
        g = fqg_ref[...] * (HEAD_DIM ** -0.5 * LOG2E)
        for h in range(HEADS):
            y = yt[h * HEAD_DIM:(h + 1) * HEAD_DIM, :]
            ms = jnp.mean(y * y, axis=0, keepdims=True)
            out_ref[h * HEAD_DIM:(h + 1) * HEAD_DIM, :] = (y * lax.rsqrt(ms + EPS) * g).astype(BF16)

    @pl.when(j == 2)
    def _():
        g = dqg_ref[...] * (DIFF_QK ** -0.5 * LOG2E)
        c = cos_ref[...]
        s = sin_ref[...]
        for h in range(HEADS):
            parts = []
            for p in range(2):
                lo = h * HEAD_DIM + p * DIFF_QK
                y = yt[lo:lo + DIFF_QK, :]
                ms = jnp.mean(y * y, axis=0, keepdims=True)
                n = y * lax.rsqrt(ms + EPS) * g
                x1 = n[0:ROPE_HALF, :]
                x2 = n[ROPE_HALF:2 * ROPE_HALF, :]
                parts += [x1 * c - x2 * s, x2 * c + x1 * s, n[2 * ROPE_HALF:, :]]
            out_ref[h * HEAD_DIM:(h + 1) * HEAD_DIM, :] = jnp.concatenate(parts, axis=0).astype(BF16)

    @pl.when((j == 1) | (j == 3))
    def _():
        out_ref[...] = yt.astype(BF16)


def _proj_t(x, g1, wt, fqg, dqg, cos_t, sin_t):
    return pl.pallas_call(
        _proj_t_kernel,
        grid=(NT, 4),
        in_specs=[
            pl.BlockSpec((T, D_MODEL), lambda i, j: (i, 0)),
            pl.BlockSpec((1, D_MODEL), lambda i, j: (0, 0)),
            pl.BlockSpec((GROUP_W, D_MODEL), lambda i, j: (j, 0)),
            pl.BlockSpec((HEAD_DIM, 1), lambda i, j: (0, 0)),
            pl.BlockSpec((DIFF_QK, 1), lambda i, j: (0, 0)),
            pl.BlockSpec((ROPE_HALF, T), lambda i, j: (0, i)),
            pl.BlockSpec((ROPE_HALF, T), lambda i, j: (0, i)),
        ],
        out_specs=pl.BlockSpec((None, None, GROUP_W, T), lambda i, j: (j, i, 0, 0)),
        out_shape=jax.ShapeDtypeStruct((4, NT, GROUP_W, T), BF16),
        scratch_shapes=[pltpu.VMEM((T, D_MODEL), BF16)],
        compiler_params=pltpu.CompilerParams(
            dimension_semantics=("arbitrary", "arbitrary"), vmem_limit_bytes=VMEM_LIMIT),
        name="proj_t",
    )(x, g1, wt, fqg, dqg, cos_t, sin_t)


def _proj_n_kernel(x_ref, g1_ref, wk_ref, wfg_ref, bf_ref, fkg_ref, dkg_ref,
                   cosf_ref, sinn_ref, sinp_ref, kf_ref, kd_ref, h_ref, carry_ref):
    i = pl.program_id(0)
    j = pl.program_id(1)

    @pl.when(j == 0)
    def _():
        h = _rms_rows(x_ref[...], g1_ref[...]).astype(BF16)
        h_ref[...] = h
        z = jnp.dot(h, wfg_ref[...], preferred_element_type=F32) + bf_ref[...]
        c = jnp.minimum(z, 0.0) - jnp.log1p(jnp.exp(-jnp.abs(z)))
        row = lax.broadcasted_iota(jnp.int32, c.shape, 0)
        shift = 1
        while shift < T:
            c = c + jnp.where(row >= shift, pltpu.roll(c, shift, axis=0), 0.0)
            shift *= 2

        @pl.when(i == 0)
        def _():
            carry_ref[...] = jnp.zeros_like(carry_ref)

        c = c + carry_ref[...]
        carry_ref[...] = c[T - 1:T, :]
        lane = lax.broadcasted_iota(jnp.int32, c.shape, 1)
        for h_idx in range(HEADS):
            col = jnp.sum(jnp.where(lane == h_idx, c, 0.0), axis=-1, keepdims=True)
            rest = jnp.broadcast_to(col * (-LOG2E), (T, LANES))
            blk = jnp.zeros((T, LANES), F32)
            for t in range(BIAS_TERMS):
                term = rest.astype(BF16).astype(F32)
                blk = jnp.where(lane == t, term, blk)
                rest = rest - term
            kf_ref[h_idx, :, HEAD_DIM:] = blk.astype(BF16)

    y = jnp.dot(h_ref[...], wk_ref[...], preferred_element_type=F32)

    @pl.when(j == 0)
    def _():
        g = fkg_ref[...]
        for h_idx in range(HEADS):
            kf_ref[h_idx, :, :HEAD_DIM] = _rms_rows(y[:, h_idx * HEAD_DIM:(h_idx + 1) * HEAD_DIM], g).astype(BF16)

    @pl.when(j == 1)
    def _():
        g = dkg_ref[...]
        cf = cosf_ref[...]
        sn = sinn_ref[...]
        sp = sinp_ref[...]
        lo = lax.broadcasted_iota(jnp.int32, (T, HEAD_DIM), 1) < DIFF_QK
        for h_idx in range(HEADS):
            yh = y[:, h_idx * HEAD_DIM:(h_idx + 1) * HEAD_DIM]
            sq = yh * yh
            ms_lo = jnp.sum(jnp.where(lo, sq, 0.0), axis=-1, keepdims=True) * (1.0 / DIFF_QK)
            ms_hi = jnp.sum(jnp.where(lo, 0.0, sq), axis=-1, keepdims=True) * (1.0 / DIFF_QK)
            r = jnp.where(lo, lax.rsqrt(ms_lo + EPS), lax.rsqrt(ms_hi + EPS))
            n = yh * r * g
            rot = (n * cf + pltpu.roll(n, HEAD_DIM - ROPE_HALF, axis=1) * sn
                   + pltpu.roll(n, ROPE_HALF, axis=1) * sp)
            kd_ref[h_idx] = rot.astype(BF16)


def _proj_n(x, g1, wk, wfg, bfg, fkg, dkg, cosf, sinn, sinp):
    row_tab = pl.BlockSpec((T, HEAD_DIM), lambda i, j: (i, 0))
    vec = lambda n: pl.BlockSpec((1, n), lambda i, j: (0, 0))
    return pl.pallas_call(
        _proj_n_kernel,
        grid=(NT, 2),
        in_specs=[
            pl.BlockSpec((T, D_MODEL), lambda i, j: (i, 0)),
            vec(D_MODEL),
            pl.BlockSpec((D_MODEL, GROUP_W), lambda i, j: (0, j)),
            pl.BlockSpec((D_MODEL, LANES), lambda i, j: (0, 0)),
            vec(LANES), vec(HEAD_DIM), vec(HEAD_DIM),
            row_tab, row_tab, row_tab,
        ],
        out_specs=[
            pl.BlockSpec((HEADS, T, 2 * HEAD_DIM), lambda i, j: (0, i, 0)),
            pl.BlockSpec((HEADS, T, HEAD_DIM), lambda i, j: (0, i, 0)),
        ],
        out_shape=[
            jax.ShapeDtypeStruct((HEADS, SEQ, 2 * HEAD_DIM), BF16),
            jax.ShapeDtypeStruct((HEADS, SEQ, HEAD_DIM), BF16),
        ],
        scratch_shapes=[pltpu.VMEM((T, D_MODEL), BF16), pltpu.VMEM((1, LANES), F32)],
        compiler_params=pltpu.CompilerParams(
            dimension_semantics=("arbitrary", "arbitrary"), vmem_limit_bytes=VMEM_LIMIT),
        name="proj_n",
    )(x, g1, wk, wfg, bfg, fkg, dkg, cosf, sinn, sinp)


def _online_softmax_step(s, vt, m_ref, l_ref, acc_ref):
    m_old = m_ref[...]
    m_new = jnp.maximum(m_old, jnp.max(s, axis=0, keepdims=True))
    p = jnp.exp2(s - m_new)
    alpha = jnp.exp2(m_old - m_new)
    l_ref[...] = alpha * l_ref[...] + jnp.sum(p, axis=0, keepdims=True)
    acc_ref[...] = alpha * acc_ref[...] + jnp.dot(vt, p.astype(BF16), preferred_element_type=F32)
    m_ref[...] = m_new


def _attend(i, scores, allowed, vt_ref, sa_ref, sb_ref, m_ref, l_ref, acc_ref):
    m_ref[...] = jnp.full_like(m_ref, NEG)
    l_ref[...] = jnp.zeros_like(l_ref)
    acc_ref[...] = jnp.zeros_like(acc_ref)

    def consume(s, kv):
        _online_softmax_step(s, vt_ref[kv], m_ref, l_ref, acc_ref)

    sa_ref[...] = scores(0)

    def body(a, carry):
        sb_ref[...] = scores(2 * a + 1)
        consume(sa_ref[...], 2 * a)
        sa_ref[...] = scores(2 * a + 2)
        consume(sb_ref[...], 2 * a + 1)
        return carry

    lax.fori_loop(0, i, body, 0)
    sb_ref[...] = scores(2 * i + 1)
    consume(jnp.where(allowed(0), sa_ref[...], NEG), 2 * i)
    consume(jnp.where(allowed(T), sb_ref[...], NEG), 2 * i + 1)


def _fox_kernel(q_ref, k_ref, vt_ref, o_ref, sa_ref, sb_ref, m_ref, l_ref, acc_ref):
    i = pl.program_id(1)
    q = jnp.concatenate([q_ref[0], q_ref[1]], axis=1)
    row = lax.broadcasted_iota(jnp.int32, q.shape, 0)
    ones = jnp.where(row < BIAS_TERMS, 1.0, 0.0).astype(BF16)
    qb = jnp.concatenate([q, ones], axis=0)

    def scores(kv):
        off = pl.multiple_of(kv * T, T)
        return jnp.dot(k_ref[pl.ds(off, T), :], qb, preferred_element_type=F32)

    def allowed(key_off):
        key = lax.broadcasted_iota(jnp.int32, (T, TQ), 0) + key_off
        qry = lax.broadcasted_iota(jnp.int32, (T, TQ), 1)
        return key <= qry

    _attend(i, scores, allowed, vt_ref, sa_ref, sb_ref, m_ref, l_ref, acc_ref)
    o = acc_ref[...] / l_ref[...]
    o_ref[...] = o.T.astype(BF16)


def _diff_kernel(q_ref, k_ref, vt_ref, lq1_ref, lk1_ref, lq2_ref, lk2_ref, sg_ref,
                 o_ref, sa_ref, sb_ref, m_ref, l_ref, acc_ref):
    i = pl.program_id(1)
    q = jnp.concatenate([q_ref[0], q_ref[1]], axis=1)
    part = lax.broadcasted_iota(jnp.int32, q.shape, 0) < DIFF_QK
    zero = jnp.zeros_like(q)
    qq = jnp.concatenate([jnp.where(part, q, zero), jnp.where(part, zero, q)], axis=1)

    def scores(kv):
        off = pl.multiple_of(kv * T, T)
        return jnp.dot(k_ref[pl.ds(off, T), :], qq, preferred_element_type=F32)

    def allowed(key_off):
        key = lax.broadcasted_iota(jnp.int32, (T, 2 * TQ), 0) + key_off
        qry = lax.broadcasted_iota(jnp.int32, (T, 2 * TQ), 1) & (TQ - 1)
        return (key >> CHUNK_SHIFT) <= (qry >> CHUNK_SHIFT)

    _attend(i, scores, allowed, vt_ref, sa_ref, sb_ref, m_ref, l_ref, acc_ref)

    lam = (jnp.exp(jnp.sum(lq1_ref[...] * lk1_ref[...], axis=-1, keepdims=True))
           - jnp.exp(jnp.sum(lq2_ref[...] * lk2_ref[...], axis=-1, keepdims=True))
           + LAMBDA_INIT)
    o_all = acc_ref[...] / l_ref[...]
    o = o_all[:, :TQ] - lam * o_all[:, TQ:]
    ms = jnp.mean(o * o, axis=0, keepdims=True)
    o = o * lax.rsqrt(ms + EPS) * (sg_ref[...] * (1.0 - LAMBDA_INIT))
    o_ref[...] = o.T.astype(BF16)


def _attn_specs(q_seg, v_seg, k_width):
    return [
        pl.BlockSpec((None, 2, HEAD_DIM, T), lambda h, i: (q_seg, i, h, 0)),
        pl.BlockSpec((None, SEQ, k_width), lambda h, i: (h, 0, 0)),
        pl.BlockSpec((None, NT, HEAD_DIM, T), lambda h, i: (v_seg, 0, h, 0)),
    ]


def _attn_scratch(width):
    return [pltpu.VMEM((T, width), F32), pltpu.VMEM((T, width), F32),
            pltpu.VMEM((1, width), F32), pltpu.VMEM((1, width), F32), pltpu.VMEM((HEAD_DIM, width), F32)]


def _fox_attn(pt, k_fox):
    return pl.pallas_call(
        _fox_kernel,
        grid=(HEADS, NQ),
        in_specs=_attn_specs(0, 1, 2 * HEAD_DIM),
        out_specs=pl.BlockSpec((TQ, HEAD_DIM), lambda h, i: (i, h)),
        out_shape=jax.ShapeDtypeStruct((SEQ, GROUP_W), BF16),
        scratch_shapes=_attn_scratch(TQ),
        compiler_params=pltpu.CompilerParams(
            dimension_semantics=("arbitrary", "arbitrary"), vmem_limit_bytes=VMEM_LIMIT),
        name="fox_attn",
    )(pt, k_fox, pt)


def _diff_attn(pt, k_diff, lq1, lk1, lq2, lk2, sg):
    lam_vec = pl.BlockSpec((1, DIFF_QK), lambda h, i: (0, 0))
    return pl.pallas_call(
        _diff_kernel,
        grid=(HEADS, NQ),
        in_specs=_attn_specs(2, 3, HEAD_DIM) + [lam_vec] * 4 + [pl.BlockSpec((HEAD_DIM, 1), lambda h, i: (0, 0))],
        out_specs=pl.BlockSpec((TQ, HEAD_DIM), lambda h, i: (i, h)),
        out_shape=jax.ShapeDtypeStruct((SEQ, GROUP_W), BF16),
        scratch_shapes=_attn_scratch(2 * TQ),
        compiler_params=pltpu.CompilerParams(
            dimension_semantics=("arbitrary", "arbitrary"), vmem_limit_bytes=VMEM_LIMIT),
        name="diff_attn",
    )(pt, k_diff, pt, lq1, lk1, lq2, lk2, sg)


def _outproj_kernel(of_ref, od_ref, wo_ref, x_ref, g2_ref, x1_ref, h2_ref):
    mix = (jnp.dot(of_ref[...], wo_ref[0:GROUP_W, :], preferred_element_type=F32)
           + jnp.dot(od_ref[...], wo_ref[GROUP_W:, :], preferred_element_type=F32))
    x1 = x_ref[...] + mix
    x1_ref[...] = x1
    h2_ref[...] = _rms_rows(x1, g2_ref[...]).astype(BF16)


def _outproj(o_f, o_d, wo, x, g2):
    return pl.pallas_call(
        _outproj_kernel,
        grid=(NT,),
        in_specs=[
            pl.BlockSpec((T, GROUP_W), lambda i: (i, 0)),
            pl.BlockSpec((T, GROUP_W), lambda i: (i, 0)),
            pl.BlockSpec((2 * GROUP_W, D_MODEL), lambda i: (0, 0)),
            pl.BlockSpec((T, D_MODEL), lambda i: (i, 0)),
            pl.BlockSpec((1, D_MODEL), lambda i: (0, 0)),
        ],
        out_specs=[pl.BlockSpec((T, D_MODEL), lambda i: (i, 0)),
                   pl.BlockSpec((T, D_MODEL), lambda i: (i, 0))],
        out_shape=[jax.ShapeDtypeStruct((SEQ, D_MODEL), F32),
                   jax.ShapeDtypeStruct((SEQ, D_MODEL), BF16)],
        compiler_params=pltpu.CompilerParams(
            dimension_semantics=("arbitrary",), vmem_limit_bytes=VMEM_LIMIT),
        name="outproj",
    )(o_f, o_d, wo, x, g2)


def _ffn_kernel(h2_ref, halo_ref, wg_ref, wv_ref, cwg_ref, cwv_ref, cbg_ref, cbv_ref, wd_ref, x1_ref,
                out_ref, he_ref, acc_ref):
    i = pl.program_id(0)
    j = pl.program_id(1)

    @pl.when(j == 0)
    def _():
        @pl.when(i == 0)
        def _():
            he_ref[0:HALO, :] = jnp.zeros((HALO, D_MODEL), BF16)

        @pl.when(i > 0)
        def _():
            he_ref[0:HALO, :] = halo_ref[...]

        he_ref[HALO:, :] = h2_ref[...]
        acc_ref[...] = jnp.zeros_like(acc_ref)

    he = he_ref[...]

    def conv(w_ref, cw_ref, cb_ref):
        u = jnp.dot(he, w_ref[...], preferred_element_type=F32)
        cw = cw_ref[...]
        u_t = u[HALO:, :]
        u_t1 = pltpu.roll(u, 1, axis=0)[HALO:, :]
        u_t2 = pltpu.roll(u, 2, axis=0)[HALO:, :]
        return cw[2:3, :] * u_t + cw[1:2, :] * u_t1 + cw[0:1, :] * u_t2 + cb_ref[...]

    gate = conv(wg_ref, cwg_ref, cbg_ref)
    val = conv(wv_ref, cwv_ref, cbv_ref)
    act = (gate * jax.nn.sigmoid(gate) * val).astype(BF16)
    acc_ref[...] += jnp.dot(act, wd_ref[...], preferred_element_type=F32)

    @pl.when(j == NF - 1)
    def _():
        out_ref[...] = x1_ref[...] + acc_ref[...]


def _ffn(h2, wg, wv, cwg, cwv, cbg, cbv, wd, x1):
    halo_blocks_per_tile = T // HALO
    return pl.pallas_call(
        _ffn_kernel,
        grid=(NT, NF),
        in_specs=[
            pl.BlockSpec((T, D_MODEL), lambda i, j: (i, 0)),
            pl.BlockSpec((HALO, D_MODEL), lambda i, j: (jnp.maximum(i * halo_blocks_per_tile - 1, 0), 0)),
            pl.BlockSpec((D_MODEL, TF), lambda i, j: (0, j)),
            pl.BlockSpec((D_MODEL, TF), lambda i, j: (0, j)),
            pl.BlockSpec((3, TF), lambda i, j: (0, j)),
            pl.BlockSpec((3, TF), lambda i, j: (0, j)),
            pl.BlockSpec((1, TF), lambda i, j: (0, j)),
            pl.BlockSpec((1, TF), lambda i, j: (0, j)),
            pl.BlockSpec((TF, D_MODEL), lambda i, j: (j, 0)),
            pl.BlockSpec((T, D_MODEL), lambda i, j: (i, 0)),
        ],
        out_specs=pl.BlockSpec((T, D_MODEL), lambda i, j: (i, 0)),
        out_shape=jax.ShapeDtypeStruct((SEQ, D_MODEL), F32),
        scratch_shapes=[pltpu.VMEM((HALO + T, D_MODEL), BF16), pltpu.VMEM((T, D_MODEL), F32)],
        compiler_params=pltpu.CompilerParams(
            dimension_semantics=("arbitrary", "arbitrary"), vmem_limit_bytes=VMEM_LIMIT),
        name="ffn",
    )(h2, h2, wg, wv, cwg, cwv, cbg, cbv, wd, x1)


def _rope_tables():
    inv_freq = ROPE_THETA ** (-jnp.arange(0, 2 * ROPE_HALF, 2, dtype=F32) / (2 * ROPE_HALF))
    ang = jnp.arange(SEQ, dtype=F32)[:, None] * inv_freq[None, :]
    cos, sin = jnp.cos(ang), jnp.sin(ang)
    ones = jnp.ones((SEQ, DIFF_QK - 2 * ROPE_HALF), F32)
    zeros = jnp.zeros((SEQ, DIFF_QK - 2 * ROPE_HALF), F32)
    z8 = jnp.zeros((SEQ, ROPE_HALF), F32)
    cosf = jnp.tile(jnp.concatenate([cos, cos, ones], axis=1), (1, 2))
    sinn = jnp.tile(jnp.concatenate([-sin, z8, zeros], axis=1), (1, 2))
    sinp = jnp.tile(jnp.concatenate([z8, sin, zeros], axis=1), (1, 2))
    return cos.T, sin.T, cosf, sinn, sinp


def kernel(x, norm1_g, w_in, b_forget, fox_q_g, fox_k_g, diff_q_g, diff_k_g, lam_q1, lam_k1, lam_q2,
           lam_k2, diff_subln_g, w_out, norm2_g, w_up, conv_w, conv_b, w_down):
    assert x.shape == (1, SEQ, D_MODEL) and w_in.shape[0] == 1
    xs = x[0]
    w = w_in[0]
    c0 = 3 * GROUP_W + HEADS
    seg = lambda a: w[:, a:a + GROUP_W]
    wt = jnp.concatenate([seg(0), seg(2 * GROUP_W), seg(c0), seg(c0 + 2 * GROUP_W)], axis=1)
    wt = wt.astype(BF16).T
    wk = jnp.concatenate([seg(GROUP_W), seg(c0 + GROUP_W)], axis=1).astype(BF16)
    wfg = jnp.pad(w[:, 3 * GROUP_W:c0], ((0, 0), (0, LANES - HEADS))).astype(BF16)
    bfg = jnp.pad(b_forget, ((0, 0), (0, LANES - HEADS)))
    cos_t, sin_t, cosf, sinn, sinp = _rope_tables()

    pt = _proj_t(xs, norm1_g, wt, fox_q_g[0][:, None], diff_q_g[0][:, None], cos_t, sin_t)
    k_fox, k_diff = _proj_n(xs, norm1_g, wk, wfg, bfg, fox_k_g, jnp.tile(diff_k_g, (1, 2)),
                            cosf, sinn, sinp)
    o_f = _fox_attn(pt, k_fox)
    o_d = _diff_attn(pt, k_diff, lam_q1, lam_k1, lam_q2, lam_k2, diff_subln_g[0][:, None])
    x1, h2 = _outproj(o_f, o_d, w_out[0].astype(BF16), xs, norm2_g)

    ff_pad = ((0, 0), (0, FF_PAD - D_FF))
    wu = w_up[0]
    wg = jnp.pad(wu[:, :D_FF], ff_pad).astype(BF16)
    wv = jnp.pad(wu[:, D_FF:], ff_pad).astype(BF16)
    cwg = jnp.pad(conv_w[0][:, :D_FF], ff_pad)
    cwv = jnp.pad(conv_w[0][:, D_FF:], ff_pad)
    cbg = jnp.pad(conv_b[:, :D_FF], ff_pad)
    cbv = jnp.pad(conv_b[:, D_FF:], ff_pad)
    wd = jnp.pad(w_down[0], ((0, FF_PAD - D_FF), (0, 0))).astype(BF16)
    out = _ffn(h2, wg, wv, cwg, cwv, cbg, cbv, wd, x1)
    return out[None]
```

```python
import math

import jax
import jax.numpy as jnp
from jax import lax
from jax.experimental import pallas as pl
from jax.experimental.pallas import tpu as pltpu

D_MODEL = 2048
SEQ = 8192
HEADS = 8
HEAD_DIM = 128
DIFF_QK = 64
ROPE_HALF = 8
ROPE_THETA = 500000.0
GROUP_W = HEADS * HEAD_DIM
D_FF = 5504
EPS = 1e-6
NEG = -1e30
LOG2E = 1.4426950408889634
LAMBDA_INIT = 0.8 - 0.6 * math.exp(-0.3 * 0)
CHUNK_SHIFT = 6

LANES = 128
T = 512
NT = SEQ // T
FF_PAD = 5632
TF = 512
NF = FF_PAD // TF
TQ = 2 * T
NQ = SEQ // TQ
BIAS_TERMS = 3
HALO = 16
VMEM_LIMIT = 56 * 1024 * 1024

F32 = jnp.float32
BF16 = jnp.bfloat16


def _rms_rows(x, g):
    ms = jnp.mean(x * x, axis=-1, keepdims=True)
    return x * lax.rsqrt(ms + EPS) * g


SEG_FQ, SEG_FK, SEG_FV, SEG_DQ, SEG_DK, SEG_DV = range(6)
PT_FQ, PT_FV, PT_DQ, PT_DV = range(4)


def _proj_kernel(x_ref, g1_ref, w_ref, wfg_ref, bf_ref, fqg_ref, fkg_ref, dqg_ref, dkg_ref,
                 cos_ref, sin_ref, cosf_ref, sinn_ref, sinp_ref,
                 pt_ref, kf_ref, kd_ref, h_ref, carry_ref):
    i = pl.program_id(0)
    j = pl.program_id(1)

    @pl.when(j == 0)
    def _():
        h = _rms_rows(x_ref[...], g1_ref[...]).astype(BF16)
        h_ref[...] = h
        z = jnp.dot(h, wfg_ref[...], preferred_element_type=F32) + bf_ref[...]
        c = jnp.minimum(z, 0.0) - jnp.log1p(jnp.exp(-jnp.abs(z)))
        row = lax.broadcasted_iota(jnp.int32, c.shape, 0)
        shift = 1
        while shift < T:
            c = c + jnp.where(row >= shift, pltpu.roll(c, shift, axis=0), 0.0)
            shift *= 2

        @pl.when(i == 0)
        def _():
            carry_ref[...] = jnp.zeros_like(carry_ref)

        c = c + carry_ref[...]
        carry_ref[...] = c[T - 1:T, :]
        lane = lax.broadcasted_iota(jnp.int32, c.shape, 1)
        for h_idx in range(HEADS):
            col = jnp.sum(jnp.where(lane == h_idx, c, 0.0), axis=-1, keepdims=True)
            rest = jnp.broadcast_to(col * (-LOG2E), (T, LANES))
            blk = jnp.zeros((T, LANES), F32)
            for t in range(BIAS_TERMS):
                term = rest.astype(BF16).astype(F32)
                blk = jnp.where(lane == t, term, blk)
                rest = rest - term
            kf_ref[h_idx, :, HEAD_DIM:] = blk.astype(BF16)

    y = jnp.dot(h_ref[...], w_ref[...], preferred_element_type=F32)
    heads = [slice(h_idx * HEAD_DIM, (h_idx + 1) * HEAD_DIM) for h_idx in range(HEADS)]

    @pl.when(j == SEG_FQ)
    def _():
        g = fqg_ref[...] * (HEAD_DIM ** -0.5 * LOG2E)
        for hs in heads:
            yt = y[:, hs].T
            ms = jnp.mean(yt * yt, axis=0, keepdims=True)
            pt_ref[hs, :] = (yt * lax.rsqrt(ms + EPS) * g).astype(BF16)

    @pl.when(j == SEG_DQ)
    def _():
        g = dqg_ref[...] * (DIFF_QK ** -0.5 * LOG2E)
        c = cos_ref[...]
        s = sin_ref[...]
        for hs in heads:
            yt = y[:, hs].T
            parts = []
            for p in range(2):
                yp = yt[p * DIFF_QK:(p + 1) * DIFF_QK, :]
                ms = jnp.mean(yp * yp, axis=0, keepdims=True)
                n = yp * lax.rsqrt(ms + EPS) * g
                x1 = n[0:ROPE_HALF, :]
                x2 = n[ROPE_HALF:2 * ROPE_HALF, :]
                parts += [x1 * c - x2 * s, x2 * c + x1 * s, n[2 * ROPE_HALF:, :]]
            pt_ref[hs, :] = jnp.concatenate(parts, axis=0).astype(BF16)

    @pl.when((j == SEG_FV) | (j == SEG_DV))
    def _():
        for hs in heads:
            pt_ref[hs, :] = y[:, hs].T.astype(BF16)

    @pl.when(j == SEG_FK)
    def _():
        g = fkg_ref[...]
        for h_idx, hs in enumerate(heads):
            kf_ref[h_idx, :, :HEAD_DIM] = _rms_rows(y[:, hs], g).astype(BF16)

    @pl.when(j == SEG_DK)
    def _():
        g = dkg_ref[...]
        cf = cosf_ref[...]
        sn = sinn_ref[...]
        sp = sinp_ref[...]
        lo = lax.broadcasted_iota(jnp.int32, (T, HEAD_DIM), 1) < DIFF_QK
        for h_idx, hs in enumerate(heads):
            yh = y[:, hs]
            sq = yh * yh
            ms_lo = jnp.sum(jnp.where(lo, sq, 0.0), axis=-1, keepdims=True) * (1.0 / DIFF_QK)
            ms_hi = jnp.sum(jnp.where(lo, 0.0, sq), axis=-1, keepdims=True) * (1.0 / DIFF_QK)
            r = jnp.where(lo, lax.rsqrt(ms_lo + EPS), lax.rsqrt(ms_hi + EPS))
            n = yh * r * g
            rot = (n * cf + pltpu.roll(n, HEAD_DIM - ROPE_HALF, axis=1) * sn
                   + pltpu.roll(n, ROPE_HALF, axis=1) * sp)
            kd_ref[h_idx] = rot.astype(BF16)


def _pt_slot(j):
    return ((j >= SEG_FV).astype(jnp.int32) + (j >= SEG_DQ).astype(jnp.int32)
            + (j >= SEG_DV).astype(jnp.int32))


def _proj(x, g1, w, wfg, bfg, fqg, fkg, dqg, dkg, cos_t, sin_t, cosf, sinn, sinp):
    row_tab = pl.BlockSpec((T, HEAD_DIM), lambda i, j: (i, 0))
    col_tab = pl.BlockSpec((ROPE_HALF, T), lambda i, j: (0, i))
    whole = lambda r, c: pl.BlockSpec((r, c), lambda i, j: (0, 0))
    return pl.pallas_call(
        _proj_kernel,
        grid=(NT, 6),
        in_specs=[
            pl.BlockSpec((T, D_MODEL), lambda i, j: (i, 0)),
            whole(1, D_MODEL),
            pl.BlockSpec((D_MODEL, GROUP_W), lambda i, j: (0, j)),
            whole(D_MODEL, LANES), whole(1, LANES),
            whole(HEAD_DIM, 1), whole(1, HEAD_DIM), whole(DIFF_QK, 1), whole(1, HEAD_DIM),
            col_tab, col_tab, row_tab, row_tab, row_tab,
        ],
        out_specs=[
            pl.BlockSpec((None, None, GROUP_W, T), lambda i, j: (_pt_slot(j), i, 0, 0)),
            pl.BlockSpec((HEADS, T, 2 * HEAD_DIM), lambda i, j: (0, i, 0)),
            pl.BlockSpec((HEADS, T, HEAD_DIM), lambda i, j: (0, i, 0)),
        ],
        out_shape=[
            jax.ShapeDtypeStruct((4, NT, GROUP_W, T), BF16),
            jax.ShapeDtypeStruct((HEADS, SEQ, 2 * HEAD_DIM), BF16),
            jax.ShapeDtypeStruct((HEADS, SEQ, HEAD_DIM), BF16),
        ],
        scratch_shapes=[pltpu.VMEM((T, D_MODEL), BF16), pltpu.VMEM((1, LANES), F32)],
        compiler_params=pltpu.CompilerParams(
            dimension_semantics=("arbitrary", "arbitrary"), vmem_limit_bytes=VMEM_LIMIT),
        name="proj",
    )(x, g1, w, wfg, bfg, fqg, fkg, dqg, dkg, cos_t, sin_t, cosf, sinn, sinp)


def _attend(i, scores, allowed, vt_ref, s_refs, cmax_refs, m_ref, l_ref, acc_ref):
    m_ref[...] = jnp.full_like(m_ref, NEG)
    l_ref[...] = jnp.zeros_like(l_ref)
    acc_ref[...] = jnp.zeros_like(acc_ref)

    def produce(b, kv, key_off=None):
        s = scores(kv)
        if key_off is not None:
            s = jnp.where(allowed(key_off), s, NEG)
        s_refs[b][...] = s
        cmax_refs[b][...] = jnp.max(s, axis=0, keepdims=True)

    def consume(b, kv):
        m_old = m_ref[...]
        m_new = jnp.maximum(m_old, cmax_refs[b][...])
        p = jnp.exp2(s_refs[b][...] - m_new)
        alpha = jnp.exp2(m_old - m_new)
        l_ref[...] = alpha * l_ref[...] + jnp.sum(p, axis=0, keepdims=True)
        acc_ref[...] = alpha * acc_ref[...] + jnp.dot(vt_ref[kv], p.astype(BF16), preferred_element_type=F32)
        m_ref[...] = m_new

    produce(0, 0)

    def body(a, carry):
        produce(1, 2 * a + 1)
        consume(0, 2 * a)
        produce(0, 2 * a + 2)
        consume(1, 2 * a + 1)
        return carry

    lax.fori_loop(0, i, body, 0)
    produce(1, 2 * i + 1, T)
    s_diag = jnp.where(allowed(0), s_refs[0][...], NEG)
    s_refs[0][...] = s_diag
    cmax_refs[0][...] = jnp.max(s_diag, axis=0, keepdims=True)
    consume(0, 2 * i)
    consume(1, 2 * i + 1)


def _fox_kernel(q_ref, k_ref, vt_ref, o_ref, sa_ref, sb_ref, ca_ref, cb_ref, m_ref, l_ref, acc_ref):
    i = pl.program_id(1)
    q = jnp.concatenate([q_ref[0], q_ref[1]], axis=1)
    row = lax.broadcasted_iota(jnp.int32, q.shape, 0)
    ones = jnp.where(row < BIAS_TERMS, 1.0, 0.0).astype(BF16)
    qb = jnp.concatenate([q, ones], axis=0)

    def scores(kv):
        off = pl.multiple_of(kv * T, T)
        return jnp.dot(k_ref[pl.ds(off, T), :], qb, preferred_element_type=F32)

    def allowed(key_off):
        key = lax.broadcasted_iota(jnp.int32, (T, TQ), 0) + key_off
        qry = lax.broadcasted_iota(jnp.int32, (T, TQ), 1)
        return key <= qry

    _attend(i, scores, allowed, vt_ref, (sa_ref, sb_ref), (ca_ref, cb_ref), m_ref, l_ref, acc_ref)
    o = acc_ref[...] / l_ref[...]
    o_ref[...] = o.T.astype(BF16)


def _diff_kernel(q_ref, k_ref, vt_ref, lq1_ref, lk1_ref, lq2_ref, lk2_ref, sg_ref,
                 o_ref, sa_ref, sb_ref, ca_ref, cb_ref, m_ref, l_ref, acc_ref):
    i = pl.program_id(1)
    q = jnp.concatenate([q_ref[0], q_ref[1]], axis=1)
    part = lax.broadcasted_iota(jnp.int32, q.shape, 0) < DIFF_QK
    zero = jnp.zeros_like(q)
    qq = jnp.concatenate([jnp.where(part, q, zero), jnp.where(part, zero, q)], axis=1)

    def scores(kv):
        off = pl.multiple_of(kv * T, T)
        return jnp.dot(k_ref[pl.ds(off, T), :], qq, preferred_element_type=F32)

    def allowed(key_off):
        key = lax.broadcasted_iota(jnp.int32, (T, 2 * TQ), 0) + key_off
        qry = lax.broadcasted_iota(jnp.int32, (T, 2 * TQ), 1) & (TQ - 1)
        return (key >> CHUNK_SHIFT) <= (qry >> CHUNK_SHIFT)

    _attend(i, scores, allowed, vt_ref, (sa_ref, sb_ref), (ca_ref, cb_ref), m_ref, l_ref, acc_ref)

    lam = (jnp.exp(jnp.sum(lq1_ref[...] * lk1_ref[...], axis=-1, keepdims=True))
           - jnp.exp(jnp.sum(lq2_ref[...] * lk2_ref[...], axis=-1, keepdims=True))
           + LAMBDA_INIT)
    o_all = acc_ref[...] / l_ref[...]
    o = o_all[:, :TQ] - lam * o_all[:, TQ:]
    ms = jnp.mean(o * o, axis=0, keepdims=True)
    o = o * lax.rsqrt(ms + EPS) * (sg_ref[...] * (1.0 - LAMBDA_INIT))
    o_ref[...] = o.T.astype(BF16)


def _attn_specs(q_seg, v_seg, k_width):
    return [
        pl.BlockSpec((None, 2, HEAD_DIM, T), lambda h, i: (q_seg, i, h, 0)),
        pl.BlockSpec((None, SEQ, k_width), lambda h, i: (h, 0, 0)),
        pl.BlockSpec((None, NT, HEAD_DIM, T), lambda h, i: (v_seg, 0, h, 0)),
    ]


def _attn_scratch(width):
    return [pltpu.VMEM((T, width), F32), pltpu.VMEM((T, width), F32),
            pltpu.VMEM((1, width), F32), pltpu.VMEM((1, width), F32),
            pltpu.VMEM((1, width), F32), pltpu.VMEM((1, width), F32), pltpu.VMEM((HEAD_DIM, width), F32)]


def _fox_attn(pt, k_fox):
    return pl.pallas_call(
        _fox_kernel,
        grid=(HEADS, NQ),
        in_specs=_attn_specs(PT_FQ, PT_FV, 2 * HEAD_DIM),
        out_specs=pl.BlockSpec((TQ, HEAD_DIM), lambda h, i: (i, h)),
        out_shape=jax.ShapeDtypeStruct((SEQ, GROUP_W), BF16),
        scratch_shapes=_attn_scratch(TQ),
        compiler_params=pltpu.CompilerParams(
            dimension_semantics=("arbitrary", "arbitrary"), vmem_limit_bytes=VMEM_LIMIT),
        name="fox_attn",
    )(pt, k_fox, pt)


def _diff_attn(pt, k_diff, lq1, lk1, lq2, lk2, sg):
    lam_vec = pl.BlockSpec((1, DIFF_QK), lambda h, i: (0, 0))
    return pl.pallas_call(
        _diff_kernel,
        grid=(HEADS, NQ),
        in_specs=_attn_specs(PT_DQ, PT_DV, HEAD_DIM) + [lam_vec] * 4 + [pl.BlockSpec((HEAD_DIM, 1), lambda h, i: (0, 0))],
        out_specs=pl.BlockSpec((TQ, HEAD_DIM), lambda h, i: (i, h)),
        out_shape=jax.ShapeDtypeStruct((SEQ, GROUP_W), BF16),
        scratch_shapes=_attn_scratch(2 * TQ),
        compiler_params=pltpu.CompilerParams(
            dimension_semantics=("arbitrary", "arbitrary"), vmem_limit_bytes=VMEM_LIMIT),
        name="diff_attn",
    )(pt, k_diff, pt, lq1, lk1, lq2, lk2, sg)


def _outproj_kernel(of_ref, od_ref, wo_ref, x_ref, g2_ref, x1_ref, h2_ref):
    mix = (jnp.dot(of_ref[...], wo_ref[0:GROUP_W, :], preferred_element_type=F32)
           + jnp.dot(od_ref[...], wo_ref[GROUP_W:, :], preferred_element_type=F32))
    x1 = x_ref[...] + mix
    x1_ref[...] = x1
    h2_ref[...] = _rms_rows(x1, g2_ref[...]).astype(BF16)


def _outproj(o_f, o_d, wo, x, g2):
    return pl.pallas_call(
        _outproj_kernel,
        grid=(NT,),
        in_specs=[
            pl.BlockSpec((T, GROUP_W), lambda i: (i, 0)),
            pl.BlockSpec((T, GROUP_W), lambda i: (i, 0)),
            pl.BlockSpec((2 * GROUP_W, D_MODEL), lambda i: (0, 0)),
            pl.BlockSpec((T, D_MODEL), lambda i: (i, 0)),
            pl.BlockSpec((1, D_MODEL), lambda i: (0, 0)),
        ],
        out_specs=[pl.BlockSpec((T, D_MODEL), lambda i: (i, 0)),
                   pl.BlockSpec((T, D_MODEL), lambda i: (i, 0))],
        out_shape=[jax.ShapeDtypeStruct((SEQ, D_MODEL), F32),
                   jax.ShapeDtypeStruct((SEQ, D_MODEL), BF16)],
        compiler_params=pltpu.CompilerParams(
            dimension_semantics=("arbitrary",), vmem_limit_bytes=VMEM_LIMIT),
        name="outproj",
    )(o_f, o_d, wo, x, g2)


def _ffn_kernel(h2_ref, halo_ref, wg_ref, wv_ref, cwg_ref, cwv_ref, cbg_ref, cbv_ref, wd_ref, x1_ref,
                out_ref, he_ref, acc_ref):
    i = pl.program_id(0)
    j = pl.program_id(1)

    @pl.when(j == 0)
    def _():
        @pl.when(i == 0)
        def _():
            he_ref[0:HALO, :] = jnp.zeros((HALO, D_MODEL), BF16)

        @pl.when(i > 0)
        def _():
            he_ref[0:HALO, :] = halo_ref[...]

        he_ref[HALO:, :] = h2_ref[...]
        acc_ref[...] = jnp.zeros_like(acc_ref)

    he = he_ref[...]

    def conv(w_ref, cw_ref, cb_ref):
        u = jnp.dot(he, w_ref[...], preferred_element_type=F32)
        cw = cw_ref[...]
        u_t = u[HALO:, :]
        u_t1 = pltpu.roll(u, 1, axis=0)[HALO:, :]
        u_t2 = pltpu.roll(u, 2, axis=0)[HALO:, :]
        return cw[2:3, :] * u_t + cw[1:2, :] * u_t1 + cw[0:1, :] * u_t2 + cb_ref[...]

    gate = conv(wg_ref, cwg_ref, cbg_ref)
    val = conv(wv_ref, cwv_ref, cbv_ref)
    act = (gate * jax.nn.sigmoid(gate) * val).astype(BF16)
    acc_ref[...] += jnp.dot(act, wd_ref[...], preferred_element_type=F32)

    @pl.when(j == NF - 1)
    def _():
        out_ref[...] = x1_ref[...] + acc_ref[...]


def _ffn(h2, w_gv, cw_gv, cb_gv, wd, x1):
    halo_blocks_per_tile = T // HALO
    return pl.pallas_call(
        _ffn_kernel,
        grid=(NT, NF),
        in_specs=[
            pl.BlockSpec((T, D_MODEL), lambda i, j: (i, 0)),
            pl.BlockSpec((HALO, D_MODEL), lambda i, j: (jnp.maximum(i * halo_blocks_per_tile - 1, 0), 0)),
            pl.BlockSpec((D_MODEL, TF), lambda i, j: (0, j)),
            pl.BlockSpec((D_MODEL, TF), lambda i, j: (0, NF + j)),
            pl.BlockSpec((3, TF), lambda i, j: (0, j)),
            pl.BlockSpec((3, TF), lambda i, j: (0, NF + j)),
            pl.BlockSpec((1, TF), lambda i, j: (0, j)),
            pl.BlockSpec((1, TF), lambda i, j: (0, NF + j)),
            pl.BlockSpec((TF, D_MODEL), lambda i, j: (j, 0)),
            pl.BlockSpec((T, D_MODEL), lambda i, j: (i, 0)),
        ],
        out_specs=pl.BlockSpec((T, D_MODEL), lambda i, j: (i, 0)),
        out_shape=jax.ShapeDtypeStruct((SEQ, D_MODEL), F32),
        scratch_shapes=[pltpu.VMEM((HALO + T, D_MODEL), BF16), pltpu.VMEM((T, D_MODEL), F32)],
        compiler_params=pltpu.CompilerParams(
            dimension_semantics=("arbitrary", "arbitrary"), vmem_limit_bytes=VMEM_LIMIT),
        name="ffn",
    )(h2, h2, w_gv, w_gv, cw_gv, cw_gv, cb_gv, cb_gv, wd, x1)


def _rope_tables():
    inv_freq = ROPE_THETA ** (-jnp.arange(0, 2 * ROPE_HALF, 2, dtype=F32) / (2 * ROPE_HALF))
    ang = jnp.arange(SEQ, dtype=F32)[:, None] * inv_freq[None, :]
    cos, sin = jnp.cos(ang), jnp.sin(ang)
    ones = jnp.ones((SEQ, DIFF_QK - 2 * ROPE_HALF), F32)
    zeros = jnp.zeros((SEQ, DIFF_QK - 2 * ROPE_HALF), F32)
    z8 = jnp.zeros((SEQ, ROPE_HALF), F32)
    cosf = jnp.tile(jnp.concatenate([cos, cos, ones], axis=1), (1, 2))
    sinn = jnp.tile(jnp.concatenate([-sin, z8, zeros], axis=1), (1, 2))
    sinp = jnp.tile(jnp.concatenate([z8, sin, zeros], axis=1), (1, 2))
    return cos.T, sin.T, cosf, sinn, sinp


def kernel(x, norm1_g, w_in, b_forget, fox_q_g, fox_k_g, diff_q_g, diff_k_g, lam_q1, lam_k1, lam_q2,
           lam_k2, diff_subln_g, w_out, norm2_g, w_up, conv_w, conv_b, w_down):
    assert x.shape == (1, SEQ, D_MODEL) and w_in.shape[0] == 1
    xs = x[0]
    w = w_in[0]
    n_fox = 3 * GROUP_W
    w_all = jnp.concatenate([w[:, :n_fox], w[:, n_fox + HEADS:]], axis=1).astype(BF16)
    wfg = jnp.pad(w[:, n_fox:n_fox + HEADS], ((0, 0), (0, LANES - HEADS))).astype(BF16)
    bfg = jnp.pad(b_forget, ((0, 0), (0, LANES - HEADS)))
    cos_t, sin_t, cosf, sinn, sinp = _rope_tables()

    pt, k_fox, k_diff = _proj(xs, norm1_g, w_all, wfg, bfg, fox_q_g[0][:, None], fox_k_g,
                              diff_q_g[0][:, None], jnp.tile(diff_k_g, (1, 2)),
                              cos_t, sin_t, cosf, sinn, sinp)
    o_f = _fox_attn(pt, k_fox)
    o_d = _diff_attn(pt, k_diff, lam_q1, lam_k1, lam_q2, lam_k2, diff_subln_g[0][:, None])
    x1, h2 = _outproj(o_f, o_d, w_out[0].astype(BF16), xs, norm2_g)

    def gate_val(a):
        r = a.shape[0]
        return jnp.pad(a.reshape(r, 2, D_FF), ((0, 0), (0, 0), (0, FF_PAD - D_FF))).reshape(r, 2 * FF_PAD)

    w_gv = gate_val(w_up[0].astype(BF16))
    wd = jnp.pad(w_down[0].astype(BF16), ((0, FF_PAD - D_FF), (0, 0)))
    out = _ffn(h2, w_gv, gate_val(conv_w[0]), gate_val(conv_b), wd, x1)
    return out[None]
```

```python
import math

import jax
import jax.numpy as jnp
from jax import lax
from jax.experimental import pallas as pl
from jax.experimental.pallas import tpu as pltpu

D_MODEL = 2048
SEQ = 8192
HEADS = 8
HEAD_DIM = 128
DIFF_QK = 64
ROPE_HALF = 8
ROPE_THETA = 500000.0
GROUP_W = HEADS * HEAD_DIM
D_FF = 5504
EPS = 1e-6
NEG = -1e30
LOG2E = 1.4426950408889634
LAMBDA_INIT = 0.8 - 0.6 * math.exp(-0.3 * 0)
CHUNK_SHIFT = 6

LANES = 128
T = 512
NT = SEQ // T
FF_PAD = 5632
TF = 512
NF = FF_PAD // TF
TQ = 2 * T
NQ = SEQ // TQ
BIAS_TERMS = 3
HALO = 16
VMEM_LIMIT = 56 * 1024 * 1024

F32 = jnp.float32
BF16 = jnp.bfloat16


def _rms_rows(x, g):
    ms = jnp.mean(x * x, axis=-1, keepdims=True)
    return x * lax.rsqrt(ms + EPS) * g


SEG_FQ, SEG_FK, SEG_FV, SEG_DQ, SEG_DK, SEG_DV = range(6)
PT_FQ, PT_FV, PT_DQ, PT_DV = range(4)


def _proj_kernel(x_ref, g1_ref, wfox_ref, wdiff_ref, wfg_ref, bf_ref, fqg_ref, fkg_ref, dqg_ref, dkg_ref,
                 cos_ref, sin_ref, pt_ref, kf_ref, kd_ref, h_ref, carry_ref):
    i = pl.program_id(0)
    j = pl.program_id(1)

    @pl.when(j == 0)
    def _():
        h = _rms_rows(x_ref[...], g1_ref[...]).astype(BF16)
        h_ref[...] = h
        z = jnp.dot(h, wfg_ref[...], preferred_element_type=F32) + bf_ref[...]
        c = jnp.minimum(z, 0.0) - jnp.log1p(jnp.exp(-jnp.abs(z)))
        row = lax.broadcasted_iota(jnp.int32, c.shape, 0)
        shift = 1
        while shift < T:
            c = c + jnp.where(row >= shift, pltpu.roll(c, shift, axis=0), 0.0)
            shift *= 2

        @pl.when(i == 0)
        def _():
            carry_ref[...] = jnp.zeros_like(carry_ref)

        c = c + carry_ref[...]
        carry_ref[...] = c[T - 1:T, :]
        lane = lax.broadcasted_iota(jnp.int32, c.shape, 1)
        for h_idx in range(HEADS):
            col = jnp.sum(jnp.where(lane == h_idx, c, 0.0), axis=-1, keepdims=True)
            rest = jnp.broadcast_to(col * (-LOG2E), (T, LANES))
            blk = jnp.zeros((T, LANES), F32)
            for t in range(BIAS_TERMS):
                term = rest.astype(BF16).astype(F32)
                blk = jnp.where(lane == t, term, blk)
                rest = rest - term
            kf_ref[h_idx, :, HEAD_DIM:] = blk.astype(BF16)

    w = jnp.where(j < SEG_DQ, wfox_ref[...], wdiff_ref[...])
    y = jnp.dot(h_ref[...], w, preferred_element_type=F32)
    heads = [slice(h_idx * HEAD_DIM, (h_idx + 1) * HEAD_DIM) for h_idx in range(HEADS)]

    def diff_qk(yt, g):
        c = cos_ref[...]
        s = sin_ref[...]
        parts = []
        for p in range(2):
            yp = yt[p * DIFF_QK:(p + 1) * DIFF_QK, :]
            ms = jnp.mean(yp * yp, axis=0, keepdims=True)
            n = yp * lax.rsqrt(ms + EPS) * g
            x1 = n[0:ROPE_HALF, :]
            x2 = n[ROPE_HALF:2 * ROPE_HALF, :]
            parts += [x1 * c - x2 * s, x2 * c + x1 * s, n[2 * ROPE_HALF:, :]]
        return jnp.concatenate(parts, axis=0)

    @pl.when(j == SEG_FQ)
    def _():
        g = fqg_ref[...] * (HEAD_DIM ** -0.5 * LOG2E)
        for hs in heads:
            yt = y[:, hs].T
            ms = jnp.mean(yt * yt, axis=0, keepdims=True)
            pt_ref[hs, :] = (yt * lax.rsqrt(ms + EPS) * g).astype(BF16)

    @pl.when(j == SEG_DQ)
    def _():
        g = dqg_ref[...] * (DIFF_QK ** -0.5 * LOG2E)
        for hs in heads:
            pt_ref[hs, :] = diff_qk(y[:, hs].T, g).astype(BF16)

    @pl.when((j == SEG_FV) | (j == SEG_DV))
    def _():
        for hs in heads:
            pt_ref[hs, :] = y[:, hs].T.astype(BF16)

    @pl.when(j == SEG_FK)
    def _():
        g = fkg_ref[...]
        for h_idx, hs in enumerate(heads):
            kf_ref[h_idx, :, :HEAD_DIM] = _rms_rows(y[:, hs], g).astype(BF16)

    @pl.when(j == SEG_DK)
    def _():
        g = dkg_ref[...]
        for h_idx, hs in enumerate(heads):
            kd_ref[h_idx] = diff_qk(y[:, hs].T, g).T.astype(BF16)


def _pt_slot(j):
    return ((j >= SEG_FV).astype(jnp.int32) + (j >= SEG_DQ).astype(jnp.int32)
            + (j >= SEG_DV).astype(jnp.int32))


def _proj(x, g1, w_fox, w_diff, wfg, bfg, fqg, fkg, dqg, dkg, cos_t, sin_t):
    col_tab = pl.BlockSpec((ROPE_HALF, T), lambda i, j: (0, i))
    whole = lambda r, c: pl.BlockSpec((r, c), lambda i, j: (0, 0))
    return pl.pallas_call(
        _proj_kernel,
        grid=(NT, 6),
        in_specs=[
            pl.BlockSpec((T, D_MODEL), lambda i, j: (i, 0)),
            whole(1, D_MODEL),
            pl.BlockSpec((D_MODEL, GROUP_W), lambda i, j: (0, jnp.minimum(j, SEG_FV))),
            pl.BlockSpec((D_MODEL, GROUP_W), lambda i, j: (0, jnp.maximum(j - SEG_DQ, 0))),
            whole(D_MODEL, LANES), whole(1, LANES),
            whole(HEAD_DIM, 1), whole(1, HEAD_DIM), whole(DIFF_QK, 1), whole(DIFF_QK, 1),
            col_tab, col_tab,
        ],
        out_specs=[
            pl.BlockSpec((None, None, GROUP_W, T), lambda i, j: (_pt_slot(j), i, 0, 0)),
            pl.BlockSpec((HEADS, T, 2 * HEAD_DIM), lambda i, j: (0, i, 0)),
            pl.BlockSpec((HEADS, T, HEAD_DIM), lambda i, j: (0, i, 0)),
        ],
        out_shape=[
            jax.ShapeDtypeStruct((4, NT, GROUP_W, T), BF16),
            jax.ShapeDtypeStruct((HEADS, SEQ, 2 * HEAD_DIM), BF16),
            jax.ShapeDtypeStruct((HEADS, SEQ, HEAD_DIM), BF16),
        ],
        scratch_shapes=[pltpu.VMEM((T, D_MODEL), BF16), pltpu.VMEM((1, LANES), F32)],
        compiler_params=pltpu.CompilerParams(
            dimension_semantics=("arbitrary", "arbitrary"), vmem_limit_bytes=VMEM_LIMIT),
        name="proj",
    )(x, g1, w_fox, w_diff, wfg, bfg, fqg, fkg, dqg, dkg, cos_t, sin_t)


def _attend(i, scores, allowed, vt_ref, s_refs, cmax_refs, m_ref, l_ref, acc_ref):
    m_ref[...] = jnp.full_like(m_ref, NEG)
    l_ref[...] = jnp.zeros_like(l_ref)
    acc_ref[...] = jnp.zeros_like(acc_ref)

    def produce(b, kv, key_off=None):
        s = scores(kv)
        if key_off is not None:
            s = jnp.where(allowed(key_off), s, NEG)
        s_refs[b][...] = s
        cmax_refs[b][...] = jnp.max(s, axis=0, keepdims=True)

    def consume(b, kv):
        m_old = m_ref[...]
        m_new = jnp.maximum(m_old, cmax_refs[b][...])
        p = jnp.exp2(s_refs[b][...] - m_new)
        alpha = jnp.exp2(m_old - m_new)
        l_ref[...] = alpha * l_ref[...] + jnp.sum(p, axis=0, keepdims=True)
        acc_ref[...] = alpha * acc_ref[...] + jnp.dot(vt_ref[kv], p.astype(BF16), preferred_element_type=F32)
        m_ref[...] = m_new

    produce(0, 0)

    def body(a, carry):
        produce(1, 2 * a + 1)
        consume(0, 2 * a)
        produce(0, 2 * a + 2)
        consume(1, 2 * a + 1)
        return carry

    lax.fori_loop(0, i, body, 0)
    produce(1, 2 * i + 1, T)
    s_diag = jnp.where(allowed(0), s_refs[0][...], NEG)
    s_refs[0][...] = s_diag
    cmax_refs[0][...] = jnp.max(s_diag, axis=0, keepdims=True)
    consume(0, 2 * i)
    consume(1, 2 * i + 1)


def _fox_kernel(q_ref, k_ref, vt_ref, o_ref, sa_ref, sb_ref, ca_ref, cb_ref, m_ref, l_ref, acc_ref):
    i = pl.program_id(1)
    q = jnp.concatenate([q_ref[0], q_ref[1]], axis=1)
    row = lax.broadcasted_iota(jnp.int32, q.shape, 0)
    ones = jnp.where(row < BIAS_TERMS, 1.0, 0.0).astype(BF16)
    qb = jnp.concatenate([q, ones], axis=0)

    def scores(kv):
        off = pl.multiple_of(kv * T, T)
        return jnp.dot(k_ref[pl.ds(off, T), :], qb, preferred_element_type=F32)

    def allowed(key_off):
        key = lax.broadcasted_iota(jnp.int32, (T, TQ), 0) + key_off
        qry = lax.broadcasted_iota(jnp.int32, (T, TQ), 1)
        return key <= qry

    _attend(i, scores, allowed, vt_ref, (sa_ref, sb_ref), (ca_ref, cb_ref), m_ref, l_ref, acc_ref)
    o = acc_ref[...] / l_ref[...]
    o_ref[...] = o.T.astype(BF16)


def _diff_kernel(q_ref, k_ref, vt_ref, lq1_ref, lk1_ref, lq2_ref, lk2_ref, sg_ref,
                 o_ref, sa_ref, sb_ref, ca_ref, cb_ref, m_ref, l_ref, acc_ref):
    i = pl.program_id(1)
    q = jnp.concatenate([q_ref[0], q_ref[1]], axis=1)
    part = lax.broadcasted_iota(jnp.int32, q.shape, 0) < DIFF_QK
    zero = jnp.zeros_like(q)
    qq = jnp.concatenate([jnp.where(part, q, zero), jnp.where(part, zero, q)], axis=1)

    def scores(kv):
        off = pl.multiple_of(kv * T, T)
        return jnp.dot(k_ref[pl.ds(off, T), :], qq, preferred_element_type=F32)

    def allowed(key_off):
        key = lax.broadcasted_iota(jnp.int32, (T, 2 * TQ), 0) + key_off
        qry = lax.broadcasted_iota(jnp.int32, (T, 2 * TQ), 1) & (TQ - 1)
        return (key >> CHUNK_SHIFT) <= (qry >> CHUNK_SHIFT)

    _attend(i, scores, allowed, vt_ref, (sa_ref, sb_ref), (ca_ref, cb_ref), m_ref, l_ref, acc_ref)

    lam = (jnp.exp(jnp.sum(lq1_ref[...] * lk1_ref[...], axis=-1, keepdims=True))
           - jnp.exp(jnp.sum(lq2_ref[...] * lk2_ref[...], axis=-1, keepdims=True))
           + LAMBDA_INIT)
    o_all = acc_ref[...] / l_ref[...]
    o = o_all[:, :TQ] - lam * o_all[:, TQ:]
    ms = jnp.mean(o * o, axis=0, keepdims=True)
    o = o * lax.rsqrt(ms + EPS) * (sg_ref[...] * (1.0 - LAMBDA_INIT))
    o_ref[...] = o.T.astype(BF16)


def _attn_specs(q_seg, v_seg, k_width):
    return [
        pl.BlockSpec((None, 2, HEAD_DIM, T), lambda h, i: (q_seg, i, h, 0)),
        pl.BlockSpec((None, SEQ, k_width), lambda h, i: (h, 0, 0)),
        pl.BlockSpec((None, NT, HEAD_DIM, T), lambda h, i: (v_seg, 0, h, 0)),
    ]


def _attn_scratch(width):
    return [pltpu.VMEM((T, width), F32), pltpu.VMEM((T, width), F32),
            pltpu.VMEM((1, width), F32), pltpu.VMEM((1, width), F32),
            pltpu.VMEM((1, width), F32), pltpu.VMEM((1, width), F32), pltpu.VMEM((HEAD_DIM, width), F32)]


def _fox_attn(pt, k_fox):
    return pl.pallas_call(
        _fox_kernel,
        grid=(HEADS, NQ),
        in_specs=_attn_specs(PT_FQ, PT_FV, 2 * HEAD_DIM),
        out_specs=pl.BlockSpec((TQ, HEAD_DIM), lambda h, i: (i, h)),
        out_shape=jax.ShapeDtypeStruct((SEQ, GROUP_W), BF16),
        scratch_shapes=_attn_scratch(TQ),
        compiler_params=pltpu.CompilerParams(
            dimension_semantics=("arbitrary", "arbitrary"), vmem_limit_bytes=VMEM_LIMIT),
        name="fox_attn",
    )(pt, k_fox, pt)


def _diff_attn(pt, k_diff, lq1, lk1, lq2, lk2, sg):
    lam_vec = pl.BlockSpec((1, DIFF_QK), lambda h, i: (0, 0))
    return pl.pallas_call(
        _diff_kernel,
        grid=(HEADS, NQ),
        in_specs=_attn_specs(PT_DQ, PT_DV, HEAD_DIM) + [lam_vec] * 4 + [pl.BlockSpec((HEAD_DIM, 1), lambda h, i: (0, 0))],
        out_specs=pl.BlockSpec((TQ, HEAD_DIM), lambda h, i: (i, h)),
        out_shape=jax.ShapeDtypeStruct((SEQ, GROUP_W), BF16),
        scratch_shapes=_attn_scratch(2 * TQ),
        compiler_params=pltpu.CompilerParams(
            dimension_semantics=("arbitrary", "arbitrary"), vmem_limit_bytes=VMEM_LIMIT),
        name="diff_attn",
    )(pt, k_diff, pt, lq1, lk1, lq2, lk2, sg)


def _outproj_kernel(of_ref, od_ref, wo_ref, x_ref, g2_ref, x1_ref, h2_ref):
    mix = (jnp.dot(of_ref[...], wo_ref[0:GROUP_W, :], preferred_element_type=F32)
           + jnp.dot(od_ref[...], wo_ref[GROUP_W:, :], preferred_element_type=F32))
    x1 = x_ref[...] + mix
    x1_ref[...] = x1
    h2_ref[...] = _rms_rows(x1, g2_ref[...]).astype(BF16)


def _outproj(o_f, o_d, wo, x, g2):
    return pl.pallas_call(
        _outproj_kernel,
        grid=(NT,),
        in_specs=[
            pl.BlockSpec((T, GROUP_W), lambda i: (i, 0)),
            pl.BlockSpec((T, GROUP_W), lambda i: (i, 0)),
            pl.BlockSpec((2 * GROUP_W, D_MODEL), lambda i: (0, 0)),
            pl.BlockSpec((T, D_MODEL), lambda i: (i, 0)),
            pl.BlockSpec((1, D_MODEL), lambda i: (0, 0)),
        ],
        out_specs=[pl.BlockSpec((T, D_MODEL), lambda i: (i, 0)),
                   pl.BlockSpec((T, D_MODEL), lambda i: (i, 0))],
        out_shape=[jax.ShapeDtypeStruct((SEQ, D_MODEL), F32),
                   jax.ShapeDtypeStruct((SEQ, D_MODEL), BF16)],
        compiler_params=pltpu.CompilerParams(
            dimension_semantics=("arbitrary",), vmem_limit_bytes=VMEM_LIMIT),
        name="outproj",
    )(o_f, o_d, wo, x, g2)


def _ffn_kernel(h2_ref, halo_ref, wg_ref, wv_ref, cwg_ref, cwv_ref, cbg_ref, cbv_ref, wd_ref, x1_ref,
                out_ref, he_ref, acc_ref):
    i = pl.program_id(0)
    j = pl.program_id(1)

    @pl.when(j == 0)
    def _():
        @pl.when(i == 0)
        def _():
            he_ref[0:HALO, :] = jnp.zeros((HALO, D_MODEL), BF16)

        @pl.when(i > 0)
        def _():
            he_ref[0:HALO, :] = halo_ref[...]

        he_ref[HALO:, :] = h2_ref[...]
        acc_ref[...] = jnp.zeros_like(acc_ref)

    he = he_ref[...]

    def conv(w_ref, cw_ref, cb_ref):
        u = jnp.dot(he, w_ref[...], preferred_element_type=F32)
        cw = cw_ref[...]
        u_t = u[HALO:, :]
        u_t1 = pltpu.roll(u, 1, axis=0)[HALO:, :]
        u_t2 = pltpu.roll(u, 2, axis=0)[HALO:, :]
        return cw[2:3, :] * u_t + cw[1:2, :] * u_t1 + cw[0:1, :] * u_t2 + cb_ref[...]

    gate = conv(wg_ref, cwg_ref, cbg_ref)
    val = conv(wv_ref, cwv_ref, cbv_ref)
    act = (gate * jax.nn.sigmoid(gate) * val).astype(BF16)
    acc_ref[...] += jnp.dot(act, wd_ref[...], preferred_element_type=F32)

    @pl.when(j == NF - 1)
    def _():
        out_ref[...] = x1_ref[...] + acc_ref[...]


def _ffn(h2, w_gv, cw_gv, cb_gv, wd, x1):
    halo_blocks_per_tile = T // HALO
    return pl.pallas_call(
        _ffn_kernel,
        grid=(NT, NF),
        in_specs=[
            pl.BlockSpec((T, D_MODEL), lambda i, j: (i, 0)),
            pl.BlockSpec((HALO, D_MODEL), lambda i, j: (jnp.maximum(i * halo_blocks_per_tile - 1, 0), 0)),
            pl.BlockSpec((D_MODEL, TF), lambda i, j: (0, j)),
            pl.BlockSpec((D_MODEL, TF), lambda i, j: (0, NF + j)),
            pl.BlockSpec((3, TF), lambda i, j: (0, j)),
            pl.BlockSpec((3, TF), lambda i, j: (0, NF + j)),
            pl.BlockSpec((1, TF), lambda i, j: (0, j)),
            pl.BlockSpec((1, TF), lambda i, j: (0, NF + j)),
            pl.BlockSpec((TF, D_MODEL), lambda i, j: (j, 0)),
            pl.BlockSpec((T, D_MODEL), lambda i, j: (i, 0)),
        ],
        out_specs=pl.BlockSpec((T, D_MODEL), lambda i, j: (i, 0)),
        out_shape=jax.ShapeDtypeStruct((SEQ, D_MODEL), F32),
        scratch_shapes=[pltpu.VMEM((HALO + T, D_MODEL), BF16), pltpu.VMEM((T, D_MODEL), F32)],
        compiler_params=pltpu.CompilerParams(
            dimension_semantics=("arbitrary", "arbitrary"), vmem_limit_bytes=VMEM_LIMIT),
        name="ffn",
    )(h2, h2, w_gv, w_gv, cw_gv, cw_gv, cb_gv, cb_gv, wd, x1)


def _rope_tables():
    inv_freq = ROPE_THETA ** (-jnp.arange(0, 2 * ROPE_HALF, 2, dtype=F32) / (2 * ROPE_HALF))
    ang = inv_freq[:, None] * jnp.arange(SEQ, dtype=F32)[None, :]
    return jnp.cos(ang), jnp.sin(ang)


def kernel(x, norm1_g, w_in, b_forget, fox_q_g, fox_k_g, diff_q_g, diff_k_g, lam_q1, lam_k1, lam_q2,
           lam_k2, diff_subln_g, w_out, norm2_g, w_up, conv_w, conv_b, w_down):
    assert x.shape == (1, SEQ, D_MODEL) and w_in.shape[0] == 1
    xs = x[0]
    w = w_in[0]
    n_fox = 3 * GROUP_W
    w_fox = w[:, :n_fox].astype(BF16)
    w_diff = w[:, n_fox + HEADS:].astype(BF16)
    wfg = jnp.pad(w[:, n_fox:n_fox + HEADS], ((0, 0), (0, LANES - HEADS))).astype(BF16)
    bfg = jnp.pad(b_forget, ((0, 0), (0, LANES - HEADS)))
    cos_t, sin_t = _rope_tables()

    pt, k_fox, k_diff = _proj(xs, norm1_g, w_fox, w_diff, wfg, bfg, fox_q_g[0][:, None], fox_k_g,
                              diff_q_g[0][:, None], diff_k_g[0][:, None], cos_t, sin_t)
    o_f = _fox_attn(pt, k_fox)
    o_d = _diff_attn(pt, k_diff, lam_q1, lam_k1, lam_q2, lam_k2, diff_subln_g[0][:, None])
    x1, h2 = _outproj(o_f, o_d, w_out[0].astype(BF16), xs, norm2_g)

    def gate_val(a):
        z = jnp.zeros((a.shape[0], FF_PAD - D_FF), a.dtype)
        return jnp.concatenate([a[:, :D_FF], z, a[:, D_FF:], z], axis=1)

    w_gv = gate_val(w_up[0].astype(BF16))
    wd = jnp.pad(w_down[0].astype(BF16), ((0, FF_PAD - D_FF), (0, 0)))
    out = _ffn(h2, w_gv, gate_val(conv_w[0]), gate_val(conv_b), wd, x1)
    return out[None]
```

```python
import math

import jax
import jax.numpy as jnp
from jax import lax
from jax.experimental import pallas as pl
from jax.experimental.pallas import tpu as pltpu

D_MODEL = 2048
SEQ = 8192
HEADS = 8
HEAD_DIM = 128
DIFF_QK = 64
ROPE_HALF = 8
ROPE_THETA = 500000.0
GROUP_W = HEADS * HEAD_DIM
D_FF = 5504
EPS = 1e-6
NEG = -1e30
LOG2E = 1.4426950408889634
LAMBDA_INIT = 0.8 - 0.6 * math.exp(-0.3 * 0)
CHUNK_SHIFT = 6

LANES = 128
T = 512
NT = SEQ // T
TF = 512
NF = -(-D_FF // TF)
TQ = 2 * T
NQ = SEQ // TQ
BIAS_TERMS = 3
HALO = 16
VMEM_LIMIT = 56 * 1024 * 1024

F32 = jnp.float32
BF16 = jnp.bfloat16


def _rms_rows(x, g):
    ms = jnp.mean(x * x, axis=-1, keepdims=True)
    return x * lax.rsqrt(ms + EPS) * g


SEG_FQ, SEG_FK, SEG_FV, SEG_DQ, SEG_DK, SEG_DV = range(6)
PT_FQ, PT_FV, PT_DQ, PT_DV = range(4)


def _proj_kernel(x_ref, g1_ref, wfox_ref, wdiff_ref, wfg_ref, bf_ref, fqg_ref, fkg_ref, dqg_ref, dkg_ref,
                 cos_ref, sin_ref, pt_ref, kf_ref, kd_ref, h_ref, carry_ref, y_ref):
    i = pl.program_id(0)
    j = pl.program_id(1)

    @pl.when(j == 0)
    def _():
        h = _rms_rows(x_ref[...], g1_ref[...]).astype(BF16)
        h_ref[...] = h
        z = jnp.dot(h, wfg_ref[...], preferred_element_type=F32) + bf_ref[...]
        c = jnp.minimum(z, 0.0) - jnp.log1p(jnp.exp(-jnp.abs(z)))
        row = lax.broadcasted_iota(jnp.int32, c.shape, 0)
        shift = 1
        while shift < T:
            c = c + jnp.where(row >= shift, pltpu.roll(c, shift, axis=0), 0.0)
            shift *= 2

        @pl.when(i == 0)
        def _():
            carry_ref[...] = jnp.zeros_like(carry_ref)

        c = c + carry_ref[...]
        carry_ref[...] = c[T - 1:T, :]
        lane = lax.broadcasted_iota(jnp.int32, c.shape, 1)
        for h_idx in range(HEADS):
            col = jnp.sum(jnp.where(lane == h_idx, c, 0.0), axis=-1, keepdims=True)
            rest = jnp.broadcast_to(col * (-LOG2E), (T, LANES))
            blk = jnp.zeros((T, LANES), F32)
            for t in range(BIAS_TERMS):
                term = rest.astype(BF16).astype(F32)
                blk = jnp.where(lane == t, term, blk)
                rest = rest - term
            kf_ref[h_idx, :, HEAD_DIM:] = blk.astype(BF16)

    @pl.when(j < SEG_DQ)
    def _():
        y_ref[...] = jnp.dot(h_ref[...], wfox_ref[...], preferred_element_type=F32)

    @pl.when(j >= SEG_DQ)
    def _():
        y_ref[...] = jnp.dot(h_ref[...], wdiff_ref[...], preferred_element_type=F32)

    y = y_ref
    heads = [slice(h_idx * HEAD_DIM, (h_idx + 1) * HEAD_DIM) for h_idx in range(HEADS)]

    def diff_qk(yt, g):
        c = cos_ref[...]
        s = sin_ref[...]
        parts = []
        for p in range(2):
            yp = yt[p * DIFF_QK:(p + 1) * DIFF_QK, :]
            ms = jnp.mean(yp * yp, axis=0, keepdims=True)
            n = yp * lax.rsqrt(ms + EPS) * g
            x1 = n[0:ROPE_HALF, :]
            x2 = n[ROPE_HALF:2 * ROPE_HALF, :]
            parts += [x1 * c - x2 * s, x2 * c + x1 * s, n[2 * ROPE_HALF:, :]]
        return jnp.concatenate(parts, axis=0)

    @pl.when(j == SEG_FQ)
    def _():
        g = fqg_ref[...] * (HEAD_DIM ** -0.5 * LOG2E)
        for hs in heads:
            yt = y[:, hs].T
            ms = jnp.mean(yt * yt, axis=0, keepdims=True)
            pt_ref[hs, :] = (yt * lax.rsqrt(ms + EPS) * g).astype(BF16)

    @pl.when(j == SEG_DQ)
    def _():
        g = dqg_ref[...] * (DIFF_QK ** -0.5 * LOG2E)
        for hs in heads:
            pt_ref[hs, :] = diff_qk(y[:, hs].T, g).astype(BF16)

    @pl.when((j == SEG_FV) | (j == SEG_DV))
    def _():
        for hs in heads:
            pt_ref[hs, :] = y[:, hs].T.astype(BF16)

    @pl.when(j == SEG_FK)
    def _():
        g = fkg_ref[...]
        for h_idx, hs in enumerate(heads):
            kf_ref[h_idx, :, :HEAD_DIM] = _rms_rows(y[:, hs], g).astype(BF16)

    @pl.when(j == SEG_DK)
    def _():
        g = dkg_ref[...]
        for h_idx, hs in enumerate(heads):
            kd_ref[h_idx] = diff_qk(y[:, hs].T, g).T.astype(BF16)


def _pt_slot(j):
    return ((j >= SEG_FV).astype(jnp.int32) + (j >= SEG_DQ).astype(jnp.int32)
            + (j >= SEG_DV).astype(jnp.int32))


def _proj(x, g1, w_fox, w_diff, wfg, bfg, fqg, fkg, dqg, dkg, cos_t, sin_t):
    col_tab = pl.BlockSpec((ROPE_HALF, T), lambda i, j: (0, i))
    whole = lambda r, c: pl.BlockSpec((r, c), lambda i, j: (0, 0))
    return pl.pallas_call(
        _proj_kernel,
        grid=(NT, 6),
        in_specs=[
            pl.BlockSpec((T, D_MODEL), lambda i, j: (i, 0)),
            whole(1, D_MODEL),
            pl.BlockSpec((D_MODEL, GROUP_W), lambda i, j: (0, jnp.minimum(j, SEG_FV))),
            pl.BlockSpec((D_MODEL, GROUP_W), lambda i, j: (0, jnp.maximum(j - SEG_DQ, 0))),
            whole(D_MODEL, LANES), whole(1, LANES),
            whole(HEAD_DIM, 1), whole(1, HEAD_DIM), whole(DIFF_QK, 1), whole(DIFF_QK, 1),
            col_tab, col_tab,
        ],
        out_specs=[
            pl.BlockSpec((None, None, GROUP_W, T), lambda i, j: (_pt_slot(j), i, 0, 0)),
            pl.BlockSpec((HEADS, T, 2 * HEAD_DIM), lambda i, j: (0, i, 0)),
            pl.BlockSpec((HEADS, T, HEAD_DIM), lambda i, j: (0, i, 0)),
        ],
        out_shape=[
            jax.ShapeDtypeStruct((4, NT, GROUP_W, T), BF16),
            jax.ShapeDtypeStruct((HEADS, SEQ, 2 * HEAD_DIM), BF16),
            jax.ShapeDtypeStruct((HEADS, SEQ, HEAD_DIM), BF16),
        ],
        scratch_shapes=[pltpu.VMEM((T, D_MODEL), BF16), pltpu.VMEM((1, LANES), F32),
                        pltpu.VMEM((T, GROUP_W), F32)],
        compiler_params=pltpu.CompilerParams(
            dimension_semantics=("arbitrary", "arbitrary"), vmem_limit_bytes=VMEM_LIMIT),
        name="proj",
    )(x, g1, w_fox, w_diff, wfg, bfg, fqg, fkg, dqg, dkg, cos_t, sin_t)


def _attend(i, scores, allowed, vt_ref, s_refs, cmax_refs, m_ref, l_ref, acc_ref):
    m_ref[...] = jnp.full_like(m_ref, NEG)
    l_ref[...] = jnp.zeros_like(l_ref)
    acc_ref[...] = jnp.zeros_like(acc_ref)

    def produce(b, kv, key_off=None):
        s = scores(kv)
        if key_off is not None:
            s = jnp.where(allowed(key_off), s, NEG)
        s_refs[b][...] = s
        cmax_refs[b][...] = jnp.max(s, axis=0, keepdims=True)

    def consume(b, kv):
        m_old = m_ref[...]
        m_new = jnp.maximum(m_old, cmax_refs[b][...])
        p = jnp.exp2(s_refs[b][...] - m_new)
        alpha = jnp.exp2(m_old - m_new)
        l_ref[...] = alpha * l_ref[...] + jnp.sum(p, axis=0, keepdims=True)
        acc_ref[...] = alpha * acc_ref[...] + jnp.dot(vt_ref[kv], p.astype(BF16), preferred_element_type=F32)
        m_ref[...] = m_new

    produce(0, 0)

    def body(a, carry):
        produce(1, 2 * a + 1)
        consume(0, 2 * a)
        produce(0, 2 * a + 2)
        consume(1, 2 * a + 1)
        return carry

    lax.fori_loop(0, i, body, 0)
    produce(1, 2 * i + 1, T)
    s_diag = jnp.where(allowed(0), s_refs[0][...], NEG)
    s_refs[0][...] = s_diag
    cmax_refs[0][...] = jnp.max(s_diag, axis=0, keepdims=True)
    consume(0, 2 * i)
    consume(1, 2 * i + 1)


def _fox_kernel(q_ref, k_ref, vt_ref, o_ref, sa_ref, sb_ref, ca_ref, cb_ref, m_ref, l_ref, acc_ref):
    i = pl.program_id(1)
    q = jnp.concatenate([q_ref[0], q_ref[1]], axis=1)
    row = lax.broadcasted_iota(jnp.int32, q.shape, 0)
    ones = jnp.where(row < BIAS_TERMS, 1.0, 0.0).astype(BF16)
    qb = jnp.concatenate([q, ones], axis=0)

    def scores(kv):
        off = pl.multiple_of(kv * T, T)
        return jnp.dot(k_ref[pl.ds(off, T), :], qb, preferred_element_type=F32)

    def allowed(key_off):
        key = lax.broadcasted_iota(jnp.int32, (T, TQ), 0) + key_off
        qry = lax.broadcasted_iota(jnp.int32, (T, TQ), 1)
        return key <= qry

    _attend(i, scores, allowed, vt_ref, (sa_ref, sb_ref), (ca_ref, cb_ref), m_ref, l_ref, acc_ref)
    o = acc_ref[...] / l_ref[...]
    o_ref[...] = o.T.astype(BF16)


def _diff_kernel(q_ref, k_ref, vt_ref, lq1_ref, lk1_ref, lq2_ref, lk2_ref, sg_ref,
                 o_ref, sa_ref, sb_ref, ca_ref, cb_ref, m_ref, l_ref, acc_ref):
    i = pl.program_id(1)
    q = jnp.concatenate([q_ref[0], q_ref[1]], axis=1)
    part = lax.broadcasted_iota(jnp.int32, q.shape, 0) < DIFF_QK
    zero = jnp.zeros_like(q)
    qq = jnp.concatenate([jnp.where(part, q, zero), jnp.where(part, zero, q)], axis=1)

    def scores(kv):
        off = pl.multiple_of(kv * T, T)
        return jnp.dot(k_ref[pl.ds(off, T), :], qq, preferred_element_type=F32)

    def allowed(key_off):
        key = lax.broadcasted_iota(jnp.int32, (T, 2 * TQ), 0) + key_off
        qry = lax.broadcasted_iota(jnp.int32, (T, 2 * TQ), 1) & (TQ - 1)
        return (key >> CHUNK_SHIFT) <= (qry >> CHUNK_SHIFT)

    _attend(i, scores, allowed, vt_ref, (sa_ref, sb_ref), (ca_ref, cb_ref), m_ref, l_ref, acc_ref)

    lam = (jnp.exp(jnp.sum(lq1_ref[...] * lk1_ref[...], axis=-1, keepdims=True))
           - jnp.exp(jnp.sum(lq2_ref[...] * lk2_ref[...], axis=-1, keepdims=True))
           + LAMBDA_INIT)
    o_all = acc_ref[...] / l_ref[...]
    o = o_all[:, :TQ] - lam * o_all[:, TQ:]
    ms = jnp.mean(o * o, axis=0, keepdims=True)
    o = o * lax.rsqrt(ms + EPS) * (sg_ref[...] * (1.0 - LAMBDA_INIT))
    o_ref[...] = o.T.astype(BF16)


def _attn_specs(q_seg, v_seg, k_width):
    return [
        pl.BlockSpec((None, 2, HEAD_DIM, T), lambda h, i: (q_seg, i, h, 0)),
        pl.BlockSpec((None, SEQ, k_width), lambda h, i: (h, 0, 0)),
        pl.BlockSpec((None, NT, HEAD_DIM, T), lambda h, i: (v_seg, 0, h, 0)),
    ]


def _attn_scratch(width):
    return [pltpu.VMEM((T, width), F32), pltpu.VMEM((T, width), F32),
            pltpu.VMEM((1, width), F32), pltpu.VMEM((1, width), F32),
            pltpu.VMEM((1, width), F32), pltpu.VMEM((1, width), F32), pltpu.VMEM((HEAD_DIM, width), F32)]


def _fox_attn(pt, k_fox):
    return pl.pallas_call(
        _fox_kernel,
        grid=(HEADS, NQ),
        in_specs=_attn_specs(PT_FQ, PT_FV, 2 * HEAD_DIM),
        out_specs=pl.BlockSpec((TQ, HEAD_DIM), lambda h, i: (i, h)),
        out_shape=jax.ShapeDtypeStruct((SEQ, GROUP_W), BF16),
        scratch_shapes=_attn_scratch(TQ),
        compiler_params=pltpu.CompilerParams(
            dimension_semantics=("arbitrary", "arbitrary"), vmem_limit_bytes=VMEM_LIMIT),
        name="fox_attn",
    )(pt, k_fox, pt)


def _diff_attn(pt, k_diff, lq1, lk1, lq2, lk2, sg):
    lam_vec = pl.BlockSpec((1, DIFF_QK), lambda h, i: (0, 0))
    return pl.pallas_call(
        _diff_kernel,
        grid=(HEADS, NQ),
        in_specs=_attn_specs(PT_DQ, PT_DV, HEAD_DIM) + [lam_vec] * 4 + [pl.BlockSpec((HEAD_DIM, 1), lambda h, i: (0, 0))],
        out_specs=pl.BlockSpec((TQ, HEAD_DIM), lambda h, i: (i, h)),
        out_shape=jax.ShapeDtypeStruct((SEQ, GROUP_W), BF16),
        scratch_shapes=_attn_scratch(2 * TQ),
        compiler_params=pltpu.CompilerParams(
            dimension_semantics=("arbitrary", "arbitrary"), vmem_limit_bytes=VMEM_LIMIT),
        name="diff_attn",
    )(pt, k_diff, pt, lq1, lk1, lq2, lk2, sg)


def _outproj_kernel(of_ref, od_ref, wo_ref, x_ref, g2_ref, x1_ref, h2_ref):
    mix = (jnp.dot(of_ref[...], wo_ref[0:GROUP_W, :], preferred_element_type=F32)
           + jnp.dot(od_ref[...], wo_ref[GROUP_W:, :], preferred_element_type=F32))
    x1 = x_ref[...] + mix
    x1_ref[...] = x1
    h2_ref[...] = _rms_rows(x1, g2_ref[...]).astype(BF16)


def _outproj(o_f, o_d, wo, x, g2):
    return pl.pallas_call(
        _outproj_kernel,
        grid=(NT,),
        in_specs=[
            pl.BlockSpec((T, GROUP_W), lambda i: (i, 0)),
            pl.BlockSpec((T, GROUP_W), lambda i: (i, 0)),
            pl.BlockSpec((2 * GROUP_W, D_MODEL), lambda i: (0, 0)),
            pl.BlockSpec((T, D_MODEL), lambda i: (i, 0)),
            pl.BlockSpec((1, D_MODEL), lambda i: (0, 0)),
        ],
        out_specs=[pl.BlockSpec((T, D_MODEL), lambda i: (i, 0)),
                   pl.BlockSpec((T, D_MODEL), lambda i: (i, 0))],
        out_shape=[jax.ShapeDtypeStruct((SEQ, D_MODEL), F32),
                   jax.ShapeDtypeStruct((SEQ, D_MODEL), BF16)],
        compiler_params=pltpu.CompilerParams(
            dimension_semantics=("arbitrary",), vmem_limit_bytes=VMEM_LIMIT),
        name="outproj",
    )(o_f, o_d, wo, x, g2)


def _ffn_kernel(h2_ref, halo_ref, wg_ref, wv_ref, cwg_ref, cwv_ref, cbg_ref, cbv_ref, wd_ref, x1_ref,
                out_ref, he_ref, acc_ref, ug_ref, uv_ref):
    i = pl.program_id(0)
    j = pl.program_id(1)

    @pl.when(j == 0)
    def _():
        @pl.when(i == 0)
        def _():
            he_ref[0:HALO, :] = jnp.zeros((HALO, D_MODEL), BF16)

        @pl.when(i > 0)
        def _():
            he_ref[0:HALO, :] = halo_ref[...]

        he_ref[HALO:, :] = h2_ref[...]
        acc_ref[...] = jnp.zeros_like(acc_ref)

    he = he_ref[...]

    def conv(w_ref, cw_ref, cb_ref, u_ref):
        u_ref[...] = jnp.dot(he, w_ref[...], preferred_element_type=F32)
        cw = cw_ref[...]
        u_t = u_ref[HALO:, :]
        u_t1 = u_ref[HALO - 1:HALO - 1 + T, :]
        u_t2 = u_ref[HALO - 2:HALO - 2 + T, :]
        return cw[2:3, :] * u_t + cw[1:2, :] * u_t1 + cw[0:1, :] * u_t2 + cb_ref[...]

    gate = conv(wg_ref, cwg_ref, cbg_ref, ug_ref)
    val = conv(wv_ref, cwv_ref, cbv_ref, uv_ref)
    n_valid = D_FF - j * TF
    act = gate * jax.nn.sigmoid(gate) * val
    act = jnp.where(lax.broadcasted_iota(jnp.int32, act.shape, 1) < n_valid, act, 0.0).astype(BF16)
    wd = wd_ref[...]
    wd = jnp.where(lax.broadcasted_iota(jnp.int32, wd.shape, 0) < n_valid, wd, jnp.zeros_like(wd))
    acc_ref[...] += jnp.dot(act, wd, preferred_element_type=F32)

    @pl.when(j == NF - 1)
    def _():
        out_ref[...] = x1_ref[...] + acc_ref[...]


def _ffn(h2, w_gate, w_val, cw_gate, cw_val, cb_gate, cb_val, wd, x1):
    halo_blocks_per_tile = T // HALO
    return pl.pallas_call(
        _ffn_kernel,
        grid=(NT, NF),
        in_specs=[
            pl.BlockSpec((T, D_MODEL), lambda i, j: (i, 0)),
            pl.BlockSpec((HALO, D_MODEL), lambda i, j: (jnp.maximum(i * halo_blocks_per_tile - 1, 0), 0)),
            pl.BlockSpec((D_MODEL, TF), lambda i, j: (0, j)),
            pl.BlockSpec((D_MODEL, TF), lambda i, j: (0, j)),
            pl.BlockSpec((3, TF), lambda i, j: (0, j)),
            pl.BlockSpec((3, TF), lambda i, j: (0, j)),
            pl.BlockSpec((1, TF), lambda i, j: (0, j)),
            pl.BlockSpec((1, TF), lambda i, j: (0, j)),
            pl.BlockSpec((TF, D_MODEL), lambda i, j: (j, 0)),
            pl.BlockSpec((T, D_MODEL), lambda i, j: (i, 0)),
        ],
        out_specs=pl.BlockSpec((T, D_MODEL), lambda i, j: (i, 0)),
        out_shape=jax.ShapeDtypeStruct((SEQ, D_MODEL), F32),
        scratch_shapes=[pltpu.VMEM((HALO + T, D_MODEL), BF16), pltpu.VMEM((T, D_MODEL), F32),
                        pltpu.VMEM((HALO + T, TF), F32), pltpu.VMEM((HALO + T, TF), F32)],
        compiler_params=pltpu.CompilerParams(
            dimension_semantics=("arbitrary", "arbitrary"), vmem_limit_bytes=VMEM_LIMIT),
        name="ffn",
    )(h2, h2, w_gate, w_val, cw_gate, cw_val, cb_gate, cb_val, wd, x1)


def _rope_tables():
    inv_freq = ROPE_THETA ** (-jnp.arange(0, 2 * ROPE_HALF, 2, dtype=F32) / (2 * ROPE_HALF))
    ang = inv_freq[:, None] * jnp.arange(SEQ, dtype=F32)[None, :]
    return jnp.cos(ang), jnp.sin(ang)


def kernel(x, norm1_g, w_in, b_forget, fox_q_g, fox_k_g, diff_q_g, diff_k_g, lam_q1, lam_k1, lam_q2,
           lam_k2, diff_subln_g, w_out, norm2_g, w_up, conv_w, conv_b, w_down):
    assert x.shape == (1, SEQ, D_MODEL) and w_in.shape[0] == 1
    xs = x[0]
    w = w_in[0]
    n_fox = 3 * GROUP_W
    w_fox = w[:, :n_fox].astype(BF16)
    w_diff = w[:, n_fox + HEADS:].astype(BF16)
    wfg = jnp.pad(w[:, n_fox:n_fox + HEADS], ((0, 0), (0, LANES - HEADS))).astype(BF16)
    bfg = jnp.pad(b_forget, ((0, 0), (0, LANES - HEADS)))
    cos_t, sin_t = _rope_tables()

    pt, k_fox, k_diff = _proj(xs, norm1_g, w_fox, w_diff, wfg, bfg, fox_q_g[0][:, None], fox_k_g,
                              diff_q_g[0][:, None], diff_k_g[0][:, None], cos_t, sin_t)
    o_f = _fox_attn(pt, k_fox)
    o_d = _diff_attn(pt, k_diff, lam_q1, lam_k1, lam_q2, lam_k2, diff_subln_g[0][:, None])
    x1, h2 = _outproj(o_f, o_d, w_out[0].astype(BF16), xs, norm2_g)

    wu = w_up[0]
    cw = conv_w[0]
    out = _ffn(h2, wu[:, :D_FF].astype(BF16), wu[:, D_FF:].astype(BF16), cw[:, :D_FF], cw[:, D_FF:],
               conv_b[:, :D_FF], conv_b[:, D_FF:], w_down[0].astype(BF16), x1)
    return out[None]
```

```python
import math

import jax
import jax.numpy as jnp
from jax import lax
from jax.experimental import pallas as pl
from jax.experimental.pallas import tpu as pltpu

D_MODEL = 2048
SEQ = 8192
HEADS = 8
HEAD_DIM = 128
DIFF_QK = 64
ROPE_HALF = 8
ROPE_THETA = 500000.0
GROUP_W = HEADS * HEAD_DIM
D_FF = 5504
EPS = 1e-6
NEG = -1e30
LOG2E = 1.4426950408889634
LAMBDA_INIT = 0.8 - 0.6 * math.exp(-0.3 * 0)
CHUNK_SHIFT = 6

LANES = 128
T = 512
NT = SEQ // T
TF = 512
NF = -(-D_FF // TF)
TQ = 2 * T
NQ = SEQ // TQ
BIAS_TERMS = 3
HALO = 16
VMEM_LIMIT = 56 * 1024 * 1024

F32 = jnp.float32
BF16 = jnp.bfloat16


def _rms_rows(x, g):
    ms = jnp.mean(x * x, axis=-1, keepdims=True)
    return x * lax.rsqrt(ms + EPS) * g


SEG_FQ, SEG_FK, SEG_FV, SEG_DQ, SEG_DK, SEG_DV = range(6)
PT_FQ, PT_FV, PT_DQ, PT_DV = range(4)


def _proj_kernel(x_ref, g1_ref, wfox_ref, wdiff_ref, wfg_ref, bf_ref, fqg_ref, fkg_ref, dqg_ref, dkg_ref,
                 cos_ref, sin_ref, pt_ref, kf_ref, kd_ref, h_ref, carry_ref, y_ref):
    i = pl.program_id(0)
    j = pl.program_id(1)

    @pl.when(j == 0)
    def _():
        h = _rms_rows(x_ref[...], g1_ref[...]).astype(BF16)
        h_ref[...] = h
        z = jnp.dot(h, wfg_ref[...], preferred_element_type=F32) + bf_ref[...]
        c = jnp.minimum(z, 0.0) - jnp.log1p(jnp.exp(-jnp.abs(z)))
        row = lax.broadcasted_iota(jnp.int32, c.shape, 0)
        shift = 1
        while shift < T:
            c = c + jnp.where(row >= shift, pltpu.roll(c, shift, axis=0), 0.0)
            shift *= 2

        @pl.when(i == 0)
        def _():
            carry_ref[...] = jnp.zeros_like(carry_ref)

        c = c + carry_ref[...]
        carry_ref[...] = c[T - 1:T, :]
        lane = lax.broadcasted_iota(jnp.int32, c.shape, 1)
        for h_idx in range(HEADS):
            col = jnp.sum(jnp.where(lane == h_idx, c, 0.0), axis=-1, keepdims=True)
            rest = jnp.broadcast_to(col * (-LOG2E), (T, LANES))
            blk = jnp.zeros((T, LANES), F32)
            for t in range(BIAS_TERMS):
                term = rest.astype(BF16).astype(F32)
                blk = jnp.where(lane == t, term, blk)
                rest = rest - term
            kf_ref[h_idx, :, HEAD_DIM:] = blk.astype(BF16)

    @pl.when(j < SEG_DQ)
    def _():
        y_ref[...] = jnp.dot(h_ref[...], wfox_ref[...], preferred_element_type=F32)

    @pl.when(j >= SEG_DQ)
    def _():
        y_ref[...] = jnp.dot(h_ref[...], wdiff_ref[...], preferred_element_type=F32)

    y = y_ref
    heads = [slice(h_idx * HEAD_DIM, (h_idx + 1) * HEAD_DIM) for h_idx in range(HEADS)]

    def diff_qk(yt, g):
        c = cos_ref[...]
        s = sin_ref[...]
        parts = []
        for p in range(2):
            yp = yt[p * DIFF_QK:(p + 1) * DIFF_QK, :]
            ms = jnp.mean(yp * yp, axis=0, keepdims=True)
            n = yp * lax.rsqrt(ms + EPS) * g
            x1 = n[0:ROPE_HALF, :]
            x2 = n[ROPE_HALF:2 * ROPE_HALF, :]
            parts += [x1 * c - x2 * s, x2 * c + x1 * s, n[2 * ROPE_HALF:, :]]
        return jnp.concatenate(parts, axis=0)

    @pl.when(j == SEG_FQ)
    def _():
        g = fqg_ref[...] * (HEAD_DIM ** -0.5 * LOG2E)
        for hs in heads:
            yt = y[:, hs].T
            ms = jnp.mean(yt * yt, axis=0, keepdims=True)
            pt_ref[hs, :] = (yt * lax.rsqrt(ms + EPS) * g).astype(BF16)

    @pl.when(j == SEG_DQ)
    def _():
        g = dqg_ref[...] * (DIFF_QK ** -0.5 * LOG2E)
        for hs in heads:
            pt_ref[hs, :] = diff_qk(y[:, hs].T, g).astype(BF16)

    @pl.when((j == SEG_FV) | (j == SEG_DV))
    def _():
        for hs in heads:
            pt_ref[hs, :] = y[:, hs].T.astype(BF16)

    @pl.when(j == SEG_FK)
    def _():
        g = fkg_ref[...]
        for h_idx, hs in enumerate(heads):
            kf_ref[h_idx, :, :HEAD_DIM] = _rms_rows(y[:, hs], g).astype(BF16)

    @pl.when(j == SEG_DK)
    def _():
        g = dkg_ref[...]
        for h_idx, hs in enumerate(heads):
            kd_ref[h_idx] = diff_qk(y[:, hs].T, g).T.astype(BF16)


def _pt_slot(j):
    return ((j >= SEG_FV).astype(jnp.int32) + (j >= SEG_DQ).astype(jnp.int32)
            + (j >= SEG_DV).astype(jnp.int32))


def _proj(x, g1, w_fox, w_diff, wfg, bfg, fqg, fkg, dqg, dkg, cos_t, sin_t):
    col_tab = pl.BlockSpec((ROPE_HALF, T), lambda i, j: (0, i))
    whole = lambda r, c: pl.BlockSpec((r, c), lambda i, j: (0, 0))
    return pl.pallas_call(
        _proj_kernel,
        grid=(NT, 6),
        in_specs=[
            pl.BlockSpec((T, D_MODEL), lambda i, j: (i, 0)),
            whole(1, D_MODEL),
            pl.BlockSpec((D_MODEL, GROUP_W), lambda i, j: (0, jnp.minimum(j, SEG_FV))),
            pl.BlockSpec((D_MODEL, GROUP_W), lambda i, j: (0, jnp.maximum(j - SEG_DQ, 0))),
            whole(D_MODEL, LANES), whole(1, LANES),
            whole(HEAD_DIM, 1), whole(1, HEAD_DIM), whole(DIFF_QK, 1), whole(DIFF_QK, 1),
            col_tab, col_tab,
        ],
        out_specs=[
            pl.BlockSpec((None, None, GROUP_W, T), lambda i, j: (_pt_slot(j), i, 0, 0)),
            pl.BlockSpec((HEADS, T, 2 * HEAD_DIM), lambda i, j: (0, i, 0)),
            pl.BlockSpec((HEADS, T, HEAD_DIM), lambda i, j: (0, i, 0)),
        ],
        out_shape=[
            jax.ShapeDtypeStruct((4, NT, GROUP_W, T), BF16),
            jax.ShapeDtypeStruct((HEADS, SEQ, 2 * HEAD_DIM), BF16),
            jax.ShapeDtypeStruct((HEADS, SEQ, HEAD_DIM), BF16),
        ],
        scratch_shapes=[pltpu.VMEM((T, D_MODEL), BF16), pltpu.VMEM((1, LANES), F32),
                        pltpu.VMEM((T, GROUP_W), F32)],
        compiler_params=pltpu.CompilerParams(
            dimension_semantics=("arbitrary", "arbitrary"), vmem_limit_bytes=VMEM_LIMIT),
        name="proj",
    )(x, g1, w_fox, w_diff, wfg, bfg, fqg, fkg, dqg, dkg, cos_t, sin_t)


def _attend(i, scores, allowed, hi_cols, vt_ref, s_refs, cmax_refs, m_ref, l_ref, acc_ref):
    m_ref[...] = jnp.full_like(m_ref, NEG)
    l_ref[...] = jnp.zeros_like(l_ref)
    acc_ref[...] = jnp.zeros_like(acc_ref)

    def produce(b, kv):
        s = scores(kv, None)
        s_refs[b][...] = s
        cmax_refs[b][...] = jnp.max(s, axis=0, keepdims=True)

    def consume(b, kv):
        m_old = m_ref[...]
        m_new = jnp.maximum(m_old, cmax_refs[b][...])
        p = jnp.exp2(s_refs[b][...] - m_new)
        alpha = jnp.exp2(m_old - m_new)
        l_ref[...] = alpha * l_ref[...] + jnp.sum(p, axis=0, keepdims=True)
        acc_ref[...] = alpha * acc_ref[...] + jnp.dot(vt_ref[kv], p.astype(BF16), preferred_element_type=F32)
        m_ref[...] = m_new

    def consume_cols(s, kv, cols):
        load = lambda ref: jnp.concatenate([ref[:, c] for c in cols], axis=1)
        m_old = load(m_ref)
        m_new = jnp.maximum(m_old, jnp.max(s, axis=0, keepdims=True))
        p = jnp.exp2(s - m_new)
        alpha = jnp.exp2(m_old - m_new)
        l_new = alpha * load(l_ref) + jnp.sum(p, axis=0, keepdims=True)
        acc_new = alpha * load(acc_ref) + jnp.dot(vt_ref[kv], p.astype(BF16), preferred_element_type=F32)
        off = 0
        for c in cols:
            part = slice(off, off + c.stop - c.start)
            m_ref[:, c], l_ref[:, c], acc_ref[:, c] = m_new[:, part], l_new[:, part], acc_new[:, part]
            off = part.stop

    produce(0, 0)

    def body(a, carry):
        produce(1, 2 * a + 1)
        consume(0, 2 * a)
        produce(0, 2 * a + 2)
        consume(1, 2 * a + 1)
        return carry

    lax.fori_loop(0, i, body, 0)
    w_hi = sum(c.stop - c.start for c in hi_cols)
    s_refs[1][:, :w_hi] = jnp.where(allowed(0, hi_cols), scores(2 * i + 1, hi_cols), NEG)
    s_diag = jnp.where(allowed(0, None), s_refs[0][...], NEG)
    s_refs[0][...] = s_diag
    cmax_refs[0][...] = jnp.max(s_diag, axis=0, keepdims=True)
    consume(0, 2 * i)
    consume_cols(s_refs[1][:, :w_hi], 2 * i + 1, hi_cols)


def _fox_kernel(q_ref, k_ref, vt_ref, o_ref, sa_ref, sb_ref, ca_ref, cb_ref, m_ref, l_ref, acc_ref):
    i = pl.program_id(1)

    def with_bias_rows(q):
        row = lax.broadcasted_iota(jnp.int32, q.shape, 0)
        ones = jnp.where(row < BIAS_TERMS, 1.0, 0.0).astype(BF16)
        return jnp.concatenate([q, ones], axis=0)

    qb = with_bias_rows(jnp.concatenate([q_ref[0], q_ref[1]], axis=1))
    hi_cols = [slice(T, TQ)]

    def scores(kv, cols):
        off = pl.multiple_of(kv * T, T)
        rhs = with_bias_rows(q_ref[1]) if cols else qb
        return jnp.dot(k_ref[pl.ds(off, T), :], rhs, preferred_element_type=F32)

    def allowed(key_off, cols):
        width = T if cols else TQ
        key = lax.broadcasted_iota(jnp.int32, (T, width), 0) + key_off
        qry = lax.broadcasted_iota(jnp.int32, (T, width), 1)
        return key <= qry

    _attend(i, scores, allowed, hi_cols, vt_ref, (sa_ref, sb_ref), (ca_ref, cb_ref), m_ref, l_ref, acc_ref)
    o = acc_ref[...] / l_ref[...]
    o_ref[...] = o.T.astype(BF16)


def _diff_kernel(q_ref, k_ref, vt_ref, lq1_ref, lk1_ref, lq2_ref, lk2_ref, sg_ref,
                 o_ref, sa_ref, sb_ref, ca_ref, cb_ref, m_ref, l_ref, acc_ref):
    i = pl.program_id(1)

    def split_parts(q):
        part = lax.broadcasted_iota(jnp.int32, q.shape, 0) < DIFF_QK
        zero = jnp.zeros_like(q)
        return jnp.concatenate([jnp.where(part, q, zero), jnp.where(part, zero, q)], axis=1)

    qq = split_parts(jnp.concatenate([q_ref[0], q_ref[1]], axis=1))
    hi_cols = [slice(T, TQ), slice(TQ + T, 2 * TQ)]

    def scores(kv, cols):
        off = pl.multiple_of(kv * T, T)
        rhs = split_parts(q_ref[1]) if cols else qq
        return jnp.dot(k_ref[pl.ds(off, T), :], rhs, preferred_element_type=F32)

    def allowed(key_off, cols):
        width = T if cols else TQ
        key = lax.broadcasted_iota(jnp.int32, (T, 2 * width), 0) + key_off
        qry = lax.broadcasted_iota(jnp.int32, (T, 2 * width), 1) & (width - 1)
        return (key >> CHUNK_SHIFT) <= (qry >> CHUNK_SHIFT)

    _attend(i, scores, allowed, hi_cols, vt_ref, (sa_ref, sb_ref), (ca_ref, cb_ref), m_ref, l_ref, acc_ref)

    lam = (jnp.exp(jnp.sum(lq1_ref[...] * lk1_ref[...], axis=-1, keepdims=True))
           - jnp.exp(jnp.sum(lq2_ref[...] * lk2_ref[...], axis=-1, keepdims=True))
           + LAMBDA_INIT)
    o_all = acc_ref[...] / l_ref[...]
    o = o_all[:, :TQ] - lam * o_all[:, TQ:]
    ms = jnp.mean(o * o, axis=0, keepdims=True)
    o = o * lax.rsqrt(ms + EPS) * (sg_ref[...] * (1.0 - LAMBDA_INIT))
    o_ref[...] = o.T.astype(BF16)


def _attn_specs(q_seg, v_seg, k_width):
    return [
        pl.BlockSpec((None, 2, HEAD_DIM, T), lambda h, i: (q_seg, i, h, 0)),
        pl.BlockSpec((None, SEQ, k_width), lambda h, i: (h, 0, 0)),
        pl.BlockSpec((None, NT, HEAD_DIM, T), lambda h, i: (v_seg, 0, h, 0)),
    ]


def _attn_scratch(width):
    return [pltpu.VMEM((T, width), F32), pltpu.VMEM((T, width), F32),
            pltpu.VMEM((1, width), F32), pltpu.VMEM((1, width), F32),
            pltpu.VMEM((1, width), F32), pltpu.VMEM((1, width), F32), pltpu.VMEM((HEAD_DIM, width), F32)]


def _fox_attn(pt, k_fox):
    return pl.pallas_call(
        _fox_kernel,
        grid=(HEADS, NQ),
        in_specs=_attn_specs(PT_FQ, PT_FV, 2 * HEAD_DIM),
        out_specs=pl.BlockSpec((TQ, HEAD_DIM), lambda h, i: (i, h)),
        out_shape=jax.ShapeDtypeStruct((SEQ, GROUP_W), BF16),
        scratch_shapes=_attn_scratch(TQ),
        compiler_params=pltpu.CompilerParams(
            dimension_semantics=("arbitrary", "arbitrary"), vmem_limit_bytes=VMEM_LIMIT),
        name="fox_attn",
    )(pt, k_fox, pt)


def _diff_attn(pt, k_diff, lq1, lk1, lq2, lk2, sg):
    lam_vec = pl.BlockSpec((1, DIFF_QK), lambda h, i: (0, 0))
    return pl.pallas_call(
        _diff_kernel,
        grid=(HEADS, NQ),
        in_specs=_attn_specs(PT_DQ, PT_DV, HEAD_DIM) + [lam_vec] * 4 + [pl.BlockSpec((HEAD_DIM, 1), lambda h, i: (0, 0))],
        out_specs=pl.BlockSpec((TQ, HEAD_DIM), lambda h, i: (i, h)),
        out_shape=jax.ShapeDtypeStruct((SEQ, GROUP_W), BF16),
        scratch_shapes=_attn_scratch(2 * TQ),
        compiler_params=pltpu.CompilerParams(
            dimension_semantics=("arbitrary", "arbitrary"), vmem_limit_bytes=VMEM_LIMIT),
        name="diff_attn",
    )(pt, k_diff, pt, lq1, lk1, lq2, lk2, sg)


def _outproj_kernel(of_ref, od_ref, wo_ref, x_ref, g2_ref, x1_ref, h2_ref):
    mix = (jnp.dot(of_ref[...], wo_ref[0:GROUP_W, :], preferred_element_type=F32)
           + jnp.dot(od_ref[...], wo_ref[GROUP_W:, :], preferred_element_type=F32))
    x1 = x_ref[...] + mix
    x1_ref[...] = x1
    h2_ref[...] = _rms_rows(x1, g2_ref[...]).astype(BF16)


def _outproj(o_f, o_d, wo, x, g2):
    return pl.pallas_call(
        _outproj_kernel,
        grid=(NT,),
        in_specs=[
            pl.BlockSpec((T, GROUP_W), lambda i: (i, 0)),
            pl.BlockSpec((T, GROUP_W), lambda i: (i, 0)),
            pl.BlockSpec((2 * GROUP_W, D_MODEL), lambda i: (0, 0)),
            pl.BlockSpec((T, D_MODEL), lambda i: (i, 0)),
            pl.BlockSpec((1, D_MODEL), lambda i: (0, 0)),
        ],
        out_specs=[pl.BlockSpec((T, D_MODEL), lambda i: (i, 0)),
                   pl.BlockSpec((T, D_MODEL), lambda i: (i, 0))],
        out_shape=[jax.ShapeDtypeStruct((SEQ, D_MODEL), F32),
                   jax.ShapeDtypeStruct((SEQ, D_MODEL), BF16)],
        compiler_params=pltpu.CompilerParams(
            dimension_semantics=("arbitrary",), vmem_limit_bytes=VMEM_LIMIT),
        name="outproj",
    )(o_f, o_d, wo, x, g2)


def _ffn_kernel(h2_ref, halo_ref, wg_ref, wv_ref, cwg_ref, cwv_ref, cbg_ref, cbv_ref, wd_ref, x1_ref,
                out_ref, he_ref, acc_ref, ug_ref, uv_ref):
    i = pl.program_id(0)
    j = pl.program_id(1)

    @pl.when(j == 0)
    def _():
        @pl.when(i == 0)
        def _():
            he_ref[0:HALO, :] = jnp.zeros((HALO, D_MODEL), BF16)

        @pl.when(i > 0)
        def _():
            he_ref[0:HALO, :] = halo_ref[...]

        he_ref[HALO:, :] = h2_ref[...]
        acc_ref[...] = jnp.zeros_like(acc_ref)

    he = he_ref[...]

    def conv(w_ref, cw_ref, cb_ref, u_ref):
        u_ref[...] = jnp.dot(he, w_ref[...], preferred_element_type=F32)
        cw = cw_ref[...]
        u_t = u_ref[HALO:, :]
        u_t1 = u_ref[HALO - 1:HALO - 1 + T, :]
        u_t2 = u_ref[HALO - 2:HALO - 2 + T, :]
        return cw[2:3, :] * u_t + cw[1:2, :] * u_t1 + cw[0:1, :] * u_t2 + cb_ref[...]

    gate = conv(wg_ref, cwg_ref, cbg_ref, ug_ref)
    val = conv(wv_ref, cwv_ref, cbv_ref, uv_ref)
    n_valid = D_FF - j * TF
    act = gate * jax.nn.sigmoid(gate) * val
    act = jnp.where(lax.broadcasted_iota(jnp.int32, act.shape, 1) < n_valid, act, 0.0).astype(BF16)
    wd = wd_ref[...]
    wd = jnp.where(lax.broadcasted_iota(jnp.int32, wd.shape, 0) < n_valid, wd, jnp.zeros_like(wd))
    acc_ref[...] += jnp.dot(act, wd, preferred_element_type=F32)

    @pl.when(j == NF - 1)
    def _():
        out_ref[...] = x1_ref[...] + acc_ref[...]


def _ffn(h2, w_gate, w_val, cw_gate, cw_val, cb_gate, cb_val, wd, x1):
    halo_blocks_per_tile = T // HALO
    return pl.pallas_call(
        _ffn_kernel,
        grid=(NT, NF),
        in_specs=[
            pl.BlockSpec((T, D_MODEL), lambda i, j: (i, 0)),
            pl.BlockSpec((HALO, D_MODEL), lambda i, j: (jnp.maximum(i * halo_blocks_per_tile - 1, 0), 0)),
            pl.BlockSpec((D_MODEL, TF), lambda i, j: (0, j)),
            pl.BlockSpec((D_MODEL, TF), lambda i, j: (0, j)),
            pl.BlockSpec((3, TF), lambda i, j: (0, j)),
            pl.BlockSpec((3, TF), lambda i, j: (0, j)),
            pl.BlockSpec((1, TF), lambda i, j: (0, j)),
            pl.BlockSpec((1, TF), lambda i, j: (0, j)),
            pl.BlockSpec((TF, D_MODEL), lambda i, j: (j, 0)),
            pl.BlockSpec((T, D_MODEL), lambda i, j: (i, 0)),
        ],
        out_specs=pl.BlockSpec((T, D_MODEL), lambda i, j: (i, 0)),
        out_shape=jax.ShapeDtypeStruct((SEQ, D_MODEL), F32),
        scratch_shapes=[pltpu.VMEM((HALO + T, D_MODEL), BF16), pltpu.VMEM((T, D_MODEL), F32),
                        pltpu.VMEM((HALO + T, TF), F32), pltpu.VMEM((HALO + T, TF), F32)],
        compiler_params=pltpu.CompilerParams(
            dimension_semantics=("arbitrary", "arbitrary"), vmem_limit_bytes=VMEM_LIMIT),
        name="ffn",
    )(h2, h2, w_gate, w_val, cw_gate, cw_val, cb_gate, cb_val, wd, x1)


def _rope_tables():
    inv_freq = ROPE_THETA ** (-jnp.arange(0, 2 * ROPE_HALF, 2, dtype=F32) / (2 * ROPE_HALF))
    ang = inv_freq[:, None] * jnp.arange(SEQ, dtype=F32)[None, :]
    return jnp.cos(ang), jnp.sin(ang)


def kernel(x, norm1_g, w_in, b_forget, fox_q_g, fox_k_g, diff_q_g, diff_k_g, lam_q1, lam_k1, lam_q2,
           lam_k2, diff_subln_g, w_out, norm2_g, w_up, conv_w, conv_b, w_down):
    assert x.shape == (1, SEQ, D_MODEL) and w_in.shape[0] == 1
    xs = x[0]
    w = w_in[0]
    n_fox = 3 * GROUP_W
    w_fox = w[:, :n_fox].astype(BF16)
    w_diff = w[:, n_fox + HEADS:].astype(BF16)
    wfg = jnp.pad(w[:, n_fox:n_fox + HEADS], ((0, 0), (0, LANES - HEADS))).astype(BF16)
    bfg = jnp.pad(b_forget, ((0, 0), (0, LANES - HEADS)))
    cos_t, sin_t = _rope_tables()

    pt, k_fox, k_diff = _proj(xs, norm1_g, w_fox, w_diff, wfg, bfg, fox_q_g[0][:, None], fox_k_g,
                              diff_q_g[0][:, None], diff_k_g[0][:, None], cos_t, sin_t)
    o_f = _fox_attn(pt, k_fox)
    o_d = _diff_attn(pt, k_diff, lam_q1, lam_k1, lam_q2, lam_k2, diff_subln_g[0][:, None])
    x1, h2 = _outproj(o_f, o_d, w_out[0].astype(BF16), xs, norm2_g)

    wu = w_up[0]
    cw = conv_w[0]
    out = _ffn(h2, wu[:, :D_FF].astype(BF16), wu[:, D_FF:].astype(BF16), cw[:, :D_FF], cw[:, D_FF:],
               conv_b[:, :D_FF], conv_b[:, D_FF:], w_down[0].astype(BF16), x1)
    return out[None]
```

```python
import math

import jax
import jax.numpy as jnp
from jax import lax
from jax.experimental import pallas as pl
from jax.experimental.pallas import tpu as pltpu

D_MODEL = 2048
SEQ = 8192
HEADS = 8
HEAD_DIM = 128
DIFF_QK = 64
ROPE_HALF = 8
ROPE_THETA = 500000.0
GROUP_W = HEADS * HEAD_DIM
D_FF = 5504
EPS = 1e-6
NEG = -1e30
LOG2E = 1.4426950408889634
LAMBDA_INIT = 0.8 - 0.6 * math.exp(-0.3 * 0)
CHUNK_SHIFT = 6

LANES = 128
T = 512
NT = SEQ // T
TF = 512
NF = -(-D_FF // TF)
TQ = 2 * T
NQ = SEQ // TQ
Q_PER_STEP = 2
BIAS_TERMS = 3
HALO = 16
VMEM_LIMIT = 56 * 1024 * 1024

F32 = jnp.float32
BF16 = jnp.bfloat16


def _rms_rows(x, g):
    ms = jnp.mean(x * x, axis=-1, keepdims=True)
    return x * lax.rsqrt(ms + EPS) * g


SEG_FQ, SEG_FK, SEG_FV, SEG_DQ, SEG_DK, SEG_DV = range(6)
PT_FQ, PT_FV, PT_DQ, PT_DV = range(4)


def _proj_kernel(x_ref, g1_ref, wfox_ref, wdiff_ref, wfg_ref, bf_ref, fqg_ref, fkg_ref, dqg_ref, dkg_ref,
                 cos_ref, sin_ref, pt_ref, kf_ref, kd_ref, h_ref, carry_ref, y_ref):
    i = pl.program_id(0)
    j = pl.program_id(1)

    @pl.when(j == 0)
    def _():
        h = _rms_rows(x_ref[...], g1_ref[...]).astype(BF16)
        h_ref[...] = h
        z = jnp.dot(h, wfg_ref[...], preferred_element_type=F32) + bf_ref[...]
        c = jnp.minimum(z, 0.0) - jnp.log1p(jnp.exp(-jnp.abs(z)))
        row = lax.broadcasted_iota(jnp.int32, c.shape, 0)
        shift = 1
        while shift < T:
            c = c + jnp.where(row >= shift, pltpu.roll(c, shift, axis=0), 0.0)
            shift *= 2

        @pl.when(i == 0)
        def _():
            carry_ref[...] = jnp.zeros_like(carry_ref)

        c = c + carry_ref[...]
        carry_ref[...] = c[T - 1:T, :]
        lane = lax.broadcasted_iota(jnp.int32, c.shape, 1)
        for h_idx in range(HEADS):
            col = jnp.sum(jnp.where(lane == h_idx, c, 0.0), axis=-1, keepdims=True)
            rest = jnp.broadcast_to(col * (-LOG2E), (T, LANES))
            blk = jnp.zeros((T, LANES), F32)
            for t in range(BIAS_TERMS):
                term = rest.astype(BF16).astype(F32)
                blk = jnp.where(lane == t, term, blk)
                rest = rest - term
            kf_ref[h_idx, :, HEAD_DIM:] = blk.astype(BF16)

    @pl.when(j < SEG_DQ)
    def _():
        y_ref[...] = jnp.dot(h_ref[...], wfox_ref[...], preferred_element_type=F32)

    @pl.when(j >= SEG_DQ)
    def _():
        y_ref[...] = jnp.dot(h_ref[...], wdiff_ref[...], preferred_element_type=F32)

    y = y_ref
    heads = [slice(h_idx * HEAD_DIM, (h_idx + 1) * HEAD_DIM) for h_idx in range(HEADS)]

    def diff_qk(yt, g):
        c = cos_ref[...]
        s = sin_ref[...]
        parts = []
        for p in range(2):
            yp = yt[p * DIFF_QK:(p + 1) * DIFF_QK, :]
            ms = jnp.mean(yp * yp, axis=0, keepdims=True)
            n = yp * lax.rsqrt(ms + EPS) * g
            x1 = n[0:ROPE_HALF, :]
            x2 = n[ROPE_HALF:2 * ROPE_HALF, :]
            parts += [x1 * c - x2 * s, x2 * c + x1 * s, n[2 * ROPE_HALF:, :]]
        return jnp.concatenate(parts, axis=0)

    @pl.when(j == SEG_FQ)
    def _():
        g = fqg_ref[...] * (HEAD_DIM ** -0.5 * LOG2E)
        for hs in heads:
            yt = y[:, hs].T
            ms = jnp.mean(yt * yt, axis=0, keepdims=True)
            pt_ref[hs, :] = (yt * lax.rsqrt(ms + EPS) * g).astype(BF16)

    @pl.when(j == SEG_DQ)
    def _():
        g = dqg_ref[...] * (DIFF_QK ** -0.5 * LOG2E)
        for hs in heads:
            pt_ref[hs, :] = diff_qk(y[:, hs].T, g).astype(BF16)

    @pl.when((j == SEG_FV) | (j == SEG_DV))
    def _():
        for hs in heads:
            pt_ref[hs, :] = y[:, hs].T.astype(BF16)

    @pl.when(j == SEG_FK)
    def _():
        g = fkg_ref[...]
        for h_idx, hs in enumerate(heads):
            kf_ref[h_idx, :, :HEAD_DIM] = _rms_rows(y[:, hs], g).astype(BF16)

    @pl.when(j == SEG_DK)
    def _():
        g = dkg_ref[...]
        for h_idx, hs in enumerate(heads):
            kd_ref[h_idx] = diff_qk(y[:, hs].T, g).T.astype(BF16)


def _pt_slot(j):
    return ((j >= SEG_FV).astype(jnp.int32) + (j >= SEG_DQ).astype(jnp.int32)
            + (j >= SEG_DV).astype(jnp.int32))


def _proj(x, g1, w_fox, w_diff, wfg, bfg, fqg, fkg, dqg, dkg, cos_t, sin_t):
    col_tab = pl.BlockSpec((ROPE_HALF, T), lambda i, j: (0, i))
    whole = lambda r, c: pl.BlockSpec((r, c), lambda i, j: (0, 0))
    return pl.pallas_call(
        _proj_kernel,
        grid=(NT, 6),
        in_specs=[
            pl.BlockSpec((T, D_MODEL), lambda i, j: (i, 0)),
            whole(1, D_MODEL),
            pl.BlockSpec((D_MODEL, GROUP_W), lambda i, j: (0, jnp.minimum(j, SEG_FV))),
            pl.BlockSpec((D_MODEL, GROUP_W), lambda i, j: (0, jnp.maximum(j - SEG_DQ, 0))),
            whole(D_MODEL, LANES), whole(1, LANES),
            whole(HEAD_DIM, 1), whole(1, HEAD_DIM), whole(DIFF_QK, 1), whole(DIFF_QK, 1),
            col_tab, col_tab,
        ],
        out_specs=[
            pl.BlockSpec((None, None, GROUP_W, T), lambda i, j: (_pt_slot(j), i, 0, 0)),
            pl.BlockSpec((HEADS, T, 2 * HEAD_DIM), lambda i, j: (0, i, 0)),
            pl.BlockSpec((HEADS, T, HEAD_DIM), lambda i, j: (0, i, 0)),
        ],
        out_shape=[
            jax.ShapeDtypeStruct((4, NT, GROUP_W, T), BF16),
            jax.ShapeDtypeStruct((HEADS, SEQ, 2 * HEAD_DIM), BF16),
            jax.ShapeDtypeStruct((HEADS, SEQ, HEAD_DIM), BF16),
        ],
        scratch_shapes=[pltpu.VMEM((T, D_MODEL), BF16), pltpu.VMEM((1, LANES), F32),
                        pltpu.VMEM((T, GROUP_W), F32)],
        compiler_params=pltpu.CompilerParams(
            dimension_semantics=("arbitrary", "arbitrary"), vmem_limit_bytes=VMEM_LIMIT),
        name="proj",
    )(x, g1, w_fox, w_diff, wfg, bfg, fqg, fkg, dqg, dkg, cos_t, sin_t)


def _attend(i, scores, allowed, hi_cols, vt_ref, s_refs, cmax_refs, m_ref, l_ref, acc_ref):
    m_ref[...] = jnp.full_like(m_ref, NEG)
    l_ref[...] = jnp.zeros_like(l_ref)
    acc_ref[...] = jnp.zeros_like(acc_ref)

    def produce(b, kv):
        s = scores(kv, None)
        s_refs[b][...] = s
        cmax_refs[b][...] = jnp.max(s, axis=0, keepdims=True)

    def consume(b, kv):
        m_old = m_ref[...]
        m_new = jnp.maximum(m_old, cmax_refs[b][...])
        p = jnp.exp2(s_refs[b][...] - m_new)
        alpha = jnp.exp2(m_old - m_new)
        l_ref[...] = alpha * l_ref[...] + jnp.sum(p, axis=0, keepdims=True)
        acc_ref[...] = alpha * acc_ref[...] + jnp.dot(vt_ref[kv], p.astype(BF16), preferred_element_type=F32)
        m_ref[...] = m_new

    def consume_cols(s, kv, cols):
        load = lambda ref: jnp.concatenate([ref[:, c] for c in cols], axis=1)
        m_old = load(m_ref)
        m_new = jnp.maximum(m_old, jnp.max(s, axis=0, keepdims=True))
        p = jnp.exp2(s - m_new)
        alpha = jnp.exp2(m_old - m_new)
        l_new = alpha * load(l_ref) + jnp.sum(p, axis=0, keepdims=True)
        acc_new = alpha * load(acc_ref) + jnp.dot(vt_ref[kv], p.astype(BF16), preferred_element_type=F32)
        off = 0
        for c in cols:
            part = slice(off, off + c.stop - c.start)
            m_ref[:, c], l_ref[:, c], acc_ref[:, c] = m_new[:, part], l_new[:, part], acc_new[:, part]
            off = part.stop

    produce(0, 0)

    def body(a, carry):
        produce(1, 2 * a + 1)
        consume(0, 2 * a)
        produce(0, 2 * a + 2)
        consume(1, 2 * a + 1)
        return carry

    lax.fori_loop(0, i, body, 0)
    w_hi = sum(c.stop - c.start for c in hi_cols)
    s_refs[1][:, :w_hi] = jnp.where(allowed(0, hi_cols), scores(2 * i + 1, hi_cols), NEG)
    s_diag = jnp.where(allowed(0, None), s_refs[0][...], NEG)
    s_refs[0][...] = s_diag
    cmax_refs[0][...] = jnp.max(s_diag, axis=0, keepdims=True)
    consume(0, 2 * i)
    consume_cols(s_refs[1][:, :w_hi], 2 * i + 1, hi_cols)


def _fox_kernel(q_ref, k_ref, vt_ref, o_ref, sa_ref, sb_ref, ca_ref, cb_ref, m_ref, l_ref, acc_ref):
    def with_bias_rows(q):
        row = lax.broadcasted_iota(jnp.int32, q.shape, 0)
        ones = jnp.where(row < BIAS_TERMS, 1.0, 0.0).astype(BF16)
        return jnp.concatenate([q, ones], axis=0)

    def allowed(key_off, cols):
        width = T if cols else TQ
        key = lax.broadcasted_iota(jnp.int32, (T, width), 0) + key_off
        qry = lax.broadcasted_iota(jnp.int32, (T, width), 1)
        return key <= qry

    hi_cols = [slice(T, TQ)]
    for sub in range(Q_PER_STEP):
        i = pl.program_id(1) * Q_PER_STEP + sub
        qb = with_bias_rows(jnp.concatenate([q_ref[2 * sub], q_ref[2 * sub + 1]], axis=1))

        def scores(kv, cols, sub=sub, qb=qb):
            off = pl.multiple_of(kv * T, T)
            rhs = with_bias_rows(q_ref[2 * sub + 1]) if cols else qb
            return jnp.dot(k_ref[pl.ds(off, T), :], rhs, preferred_element_type=F32)

        _attend(i, scores, allowed, hi_cols, vt_ref, (sa_ref, sb_ref), (ca_ref, cb_ref), m_ref, l_ref, acc_ref)
        o = acc_ref[...] / l_ref[...]
        o_ref[sub * TQ:(sub + 1) * TQ, :] = o.T.astype(BF16)


def _diff_kernel(q_ref, k_ref, vt_ref, lq1_ref, lk1_ref, lq2_ref, lk2_ref, sg_ref,
                 o_ref, sa_ref, sb_ref, ca_ref, cb_ref, m_ref, l_ref, acc_ref):
    def split_parts(q):
        part = lax.broadcasted_iota(jnp.int32, q.shape, 0) < DIFF_QK
        zero = jnp.zeros_like(q)
        return jnp.concatenate([jnp.where(part, q, zero), jnp.where(part, zero, q)], axis=1)

    def allowed(key_off, cols):
        width = T if cols else TQ
        key = lax.broadcasted_iota(jnp.int32, (T, 2 * width), 0) + key_off
        qry = lax.broadcasted_iota(jnp.int32, (T, 2 * width), 1) & (width - 1)
        return (key >> CHUNK_SHIFT) <= (qry >> CHUNK_SHIFT)

    hi_cols = [slice(T, TQ), slice(TQ + T, 2 * TQ)]
    lam = (jnp.exp(jnp.sum(lq1_ref[...] * lk1_ref[...], axis=-1, keepdims=True))
           - jnp.exp(jnp.sum(lq2_ref[...] * lk2_ref[...], axis=-1, keepdims=True))
           + LAMBDA_INIT)
    for sub in range(Q_PER_STEP):
        i = pl.program_id(1) * Q_PER_STEP + sub
        qq = split_parts(jnp.concatenate([q_ref[2 * sub], q_ref[2 * sub + 1]], axis=1))

        def scores(kv, cols, sub=sub, qq=qq):
            off = pl.multiple_of(kv * T, T)
            rhs = split_parts(q_ref[2 * sub + 1]) if cols else qq
            return jnp.dot(k_ref[pl.ds(off, T), :], rhs, preferred_element_type=F32)

        _attend(i, scores, allowed, hi_cols, vt_ref, (sa_ref, sb_ref), (ca_ref, cb_ref), m_ref, l_ref, acc_ref)
        o_all = acc_ref[...] / l_ref[...]
        o = o_all[:, :TQ] - lam * o_all[:, TQ:]
        ms = jnp.mean(o * o, axis=0, keepdims=True)
        o = o * lax.rsqrt(ms + EPS) * (sg_ref[...] * (1.0 - LAMBDA_INIT))
        o_ref[sub * TQ:(sub + 1) * TQ, :] = o.T.astype(BF16)


def _attn_specs(q_seg, v_seg, k_width):
    return [
        pl.BlockSpec((None, 2 * Q_PER_STEP, HEAD_DIM, T), lambda h, i: (q_seg, i, h, 0)),
        pl.BlockSpec((None, SEQ, k_width), lambda h, i: (h, 0, 0)),
        pl.BlockSpec((None, NT, HEAD_DIM, T), lambda h, i: (v_seg, 0, h, 0)),
    ]


def _attn_scratch(width):
    return [pltpu.VMEM((T, width), F32), pltpu.VMEM((T, width), F32),
            pltpu.VMEM((1, width), F32), pltpu.VMEM((1, width), F32),
            pltpu.VMEM((1, width), F32), pltpu.VMEM((1, width), F32), pltpu.VMEM((HEAD_DIM, width), F32)]


def _fox_attn(pt, k_fox):
    return pl.pallas_call(
        _fox_kernel,
        grid=(HEADS, NQ // Q_PER_STEP),
        in_specs=_attn_specs(PT_FQ, PT_FV, 2 * HEAD_DIM),
        out_specs=pl.BlockSpec((Q_PER_STEP * TQ, HEAD_DIM), lambda h, i: (i, h)),
        out_shape=jax.ShapeDtypeStruct((SEQ, GROUP_W), BF16),
        scratch_shapes=_attn_scratch(TQ),
        compiler_params=pltpu.CompilerParams(
            dimension_semantics=("arbitrary", "arbitrary"), vmem_limit_bytes=VMEM_LIMIT),
        name="fox_attn",
    )(pt, k_fox, pt)


def _diff_attn(pt, k_diff, lq1, lk1, lq2, lk2, sg):
    lam_vec = pl.BlockSpec((1, DIFF_QK), lambda h, i: (0, 0))
    return pl.pallas_call(
        _diff_kernel,
        grid=(HEADS, NQ // Q_PER_STEP),
        in_specs=_attn_specs(PT_DQ, PT_DV, HEAD_DIM) + [lam_vec] * 4 + [pl.BlockSpec((HEAD_DIM, 1), lambda h, i: (0, 0))],
        out_specs=pl.BlockSpec((Q_PER_STEP * TQ, HEAD_DIM), lambda h, i: (i, h)),
        out_shape=jax.ShapeDtypeStruct((SEQ, GROUP_W), BF16),
        scratch_shapes=_attn_scratch(2 * TQ),
        compiler_params=pltpu.CompilerParams(
            dimension_semantics=("arbitrary", "arbitrary"), vmem_limit_bytes=VMEM_LIMIT),
        name="diff_attn",
    )(pt, k_diff, pt, lq1, lk1, lq2, lk2, sg)


def _outproj_kernel(of_ref, od_ref, wo_ref, x_ref, g2_ref, x1_ref, h2_ref):
    mix = (jnp.dot(of_ref[...], wo_ref[0:GROUP_W, :], preferred_element_type=F32)
           + jnp.dot(od_ref[...], wo_ref[GROUP_W:, :], preferred_element_type=F32))
    x1 = x_ref[...] + mix
    x1_ref[...] = x1
    h2_ref[...] = _rms_rows(x1, g2_ref[...]).astype(BF16)


def _outproj(o_f, o_d, wo, x, g2):
    return pl.pallas_call(
        _outproj_kernel,
        grid=(NT,),
        in_specs=[
            pl.BlockSpec((T, GROUP_W), lambda i: (i, 0)),
            pl.BlockSpec((T, GROUP_W), lambda i: (i, 0)),
            pl.BlockSpec((2 * GROUP_W, D_MODEL), lambda i: (0, 0)),
            pl.BlockSpec((T, D_MODEL), lambda i: (i, 0)),
            pl.BlockSpec((1, D_MODEL), lambda i: (0, 0)),
        ],
        out_specs=[pl.BlockSpec((T, D_MODEL), lambda i: (i, 0)),
                   pl.BlockSpec((T, D_MODEL), lambda i: (i, 0))],
        out_shape=[jax.ShapeDtypeStruct((SEQ, D_MODEL), F32),
                   jax.ShapeDtypeStruct((SEQ, D_MODEL), BF16)],
        compiler_params=pltpu.CompilerParams(
            dimension_semantics=("arbitrary",), vmem_limit_bytes=VMEM_LIMIT),
        name="outproj",
    )(o_f, o_d, wo, x, g2)


def _ffn_kernel(h2_ref, halo_ref, wg_ref, wv_ref, cwg_ref, cwv_ref, cbg_ref, cbv_ref, wd_ref, x1_ref,
                out_ref, he_ref, acc_ref, ug_ref, uv_ref):
    i = pl.program_id(0)
    j = pl.program_id(1)

    @pl.when(j == 0)
    def _():
        @pl.when(i == 0)
        def _():
            he_ref[0:HALO, :] = jnp.zeros((HALO, D_MODEL), BF16)

        @pl.when(i > 0)
        def _():
            he_ref[0:HALO, :] = halo_ref[...]

        he_ref[HALO:, :] = h2_ref[...]
        acc_ref[...] = jnp.zeros_like(acc_ref)

    he = he_ref[...]

    def conv(w_ref, cw_ref, cb_ref, u_ref):
        u_ref[...] = jnp.dot(he, w_ref[...], preferred_element_type=F32)
        cw = cw_ref[...]
        u_t = u_ref[HALO:, :]
        u_t1 = u_ref[HALO - 1:HALO - 1 + T, :]
        u_t2 = u_ref[HALO - 2:HALO - 2 + T, :]
        return cw[2:3, :] * u_t + cw[1:2, :] * u_t1 + cw[0:1, :] * u_t2 + cb_ref[...]

    gate = conv(wg_ref, cwg_ref, cbg_ref, ug_ref)
    val = conv(wv_ref, cwv_ref, cbv_ref, uv_ref)
    n_valid = D_FF - j * TF
    act = gate * jax.nn.sigmoid(gate) * val
    act = jnp.where(lax.broadcasted_iota(jnp.int32, act.shape, 1) < n_valid, act, 0.0).astype(BF16)
    wd = wd_ref[...]
    wd = jnp.where(lax.broadcasted_iota(jnp.int32, wd.shape, 0) < n_valid, wd, jnp.zeros_like(wd))
    acc_ref[...] += jnp.dot(act, wd, preferred_element_type=F32)

    @pl.when(j == NF - 1)
    def _():
        out_ref[...] = x1_ref[...] + acc_ref[...]


def _ffn(h2, w_gate, w_val, cw_gate, cw_val, cb_gate, cb_val, wd, x1):
    halo_blocks_per_tile = T // HALO
    return pl.pallas_call(
        _ffn_kernel,
        grid=(NT, NF),
        in_specs=[
            pl.BlockSpec((T, D_MODEL), lambda i, j: (i, 0)),
            pl.BlockSpec((HALO, D_MODEL), lambda i, j: (jnp.maximum(i * halo_blocks_per_tile - 1, 0), 0)),
            pl.BlockSpec((D_MODEL, TF), lambda i, j: (0, j)),
            pl.BlockSpec((D_MODEL, TF), lambda i, j: (0, j)),
            pl.BlockSpec((3, TF), lambda i, j: (0, j)),
            pl.BlockSpec((3, TF), lambda i, j: (0, j)),
            pl.BlockSpec((1, TF), lambda i, j: (0, j)),
            pl.BlockSpec((1, TF), lambda i, j: (0, j)),
            pl.BlockSpec((TF, D_MODEL), lambda i, j: (j, 0)),
            pl.BlockSpec((T, D_MODEL), lambda i, j: (i, 0)),
        ],
        out_specs=pl.BlockSpec((T, D_MODEL), lambda i, j: (i, 0)),
        out_shape=jax.ShapeDtypeStruct((SEQ, D_MODEL), F32),
        scratch_shapes=[pltpu.VMEM((HALO + T, D_MODEL), BF16), pltpu.VMEM((T, D_MODEL), F32),
                        pltpu.VMEM((HALO + T, TF), F32), pltpu.VMEM((HALO + T, TF), F32)],
        compiler_params=pltpu.CompilerParams(
            dimension_semantics=("arbitrary", "arbitrary"), vmem_limit_bytes=VMEM_LIMIT),
        name="ffn",
    )(h2, h2, w_gate, w_val, cw_gate, cw_val, cb_gate, cb_val, wd, x1)


def _rope_tables():
    inv_freq = ROPE_THETA ** (-jnp.arange(0, 2 * ROPE_HALF, 2, dtype=F32) / (2 * ROPE_HALF))
    ang = inv_freq[:, None] * jnp.arange(SEQ, dtype=F32)[None, :]
    return jnp.cos(ang), jnp.sin(ang)


def kernel(x, norm1_g, w_in, b_forget, fox_q_g, fox_k_g, diff_q_g, diff_k_g, lam_q1, lam_k1, lam_q2,
           lam_k2, diff_subln_g, w_out, norm2_g, w_up, conv_w, conv_b, w_down):
    assert x.shape == (1, SEQ, D_MODEL) and w_in.shape[0] == 1
    xs = x[0]
    w = w_in[0]
    n_fox = 3 * GROUP_W
    w_fox = w[:, :n_fox].astype(BF16)
    w_diff = w[:, n_fox + HEADS:].astype(BF16)
    wfg = jnp.pad(w[:, n_fox:n_fox + HEADS], ((0, 0), (0, LANES - HEADS))).astype(BF16)
    bfg = jnp.pad(b_forget, ((0, 0), (0, LANES - HEADS)))
    cos_t, sin_t = _rope_tables()

    pt, k_fox, k_diff = _proj(xs, norm1_g, w_fox, w_diff, wfg, bfg, fox_q_g[0][:, None], fox_k_g,
                              diff_q_g[0][:, None], diff_k_g[0][:, None], cos_t, sin_t)
    o_f = _fox_attn(pt, k_fox)
    o_d = _diff_attn(pt, k_diff, lam_q1, lam_k1, lam_q2, lam_k2, diff_subln_g[0][:, None])
    x1, h2 = _outproj(o_f, o_d, w_out[0].astype(BF16), xs, norm2_g)

    wu = w_up[0]
    cw = conv_w[0]
    out = _ffn(h2, wu[:, :D_FF].astype(BF16), wu[:, D_FF:].astype(BF16), cw[:, :D_FF], cw[:, D_FF:],
               conv_b[:, :D_FF], conv_b[:, D_FF:], w_down[0].astype(BF16), x1)
    return out[None]
```

```python
import math

import jax
import jax.numpy as jnp
from jax import lax
from jax.experimental import pallas as pl
from jax.experimental.pallas import tpu as pltpu

D_MODEL = 2048
SEQ = 8192
HEADS = 8
HEAD_DIM = 128
DIFF_QK = 64
ROPE_HALF = 8
ROPE_THETA = 500000.0
GROUP_W = HEADS * HEAD_DIM
D_FF = 5504
EPS = 1e-6
NEG = -1e30
LOG2E = 1.4426950408889634
LAMBDA_INIT = 0.8 - 0.6 * math.exp(-0.3 * 0)
CHUNK_SHIFT = 6

LANES = 128
T = 512
NT = SEQ // T
TF = 512
NF = -(-D_FF // TF)
TQ = 2 * T
NQ = SEQ // TQ
Q_PER_STEP = 2
SUM_ROWS = 16
BIAS_TERMS = 3
HALO = 16
VMEM_LIMIT = 56 * 1024 * 1024

F32 = jnp.float32
BF16 = jnp.bfloat16


def _rms_rows(x, g):
    ms = jnp.mean(x * x, axis=-1, keepdims=True)
    return x * lax.rsqrt(ms + EPS) * g


SEG_FQ, SEG_FK, SEG_FV, SEG_DQ, SEG_DK, SEG_DV = range(6)
PT_FQ, PT_FV, PT_DQ, PT_DV = range(4)


def _proj_kernel(x_ref, g1_ref, wfox_ref, wdiff_ref, wfg_ref, bf_ref, fqg_ref, fkg_ref, dqg_ref, dkg_ref,
                 cos_ref, sin_ref, pt_ref, kf_ref, kd_ref, h_ref, carry_ref, y_ref):
    i = pl.program_id(0)
    j = pl.program_id(1)

    @pl.when(j == 0)
    def _():
        h = _rms_rows(x_ref[...], g1_ref[...]).astype(BF16)
        h_ref[...] = h
        z = jnp.dot(h, wfg_ref[...], preferred_element_type=F32) + bf_ref[...]
        c = jnp.minimum(z, 0.0) - jnp.log1p(jnp.exp(-jnp.abs(z)))
        row = lax.broadcasted_iota(jnp.int32, c.shape, 0)
        shift = 1
        while shift < T:
            c = c + jnp.where(row >= shift, pltpu.roll(c, shift, axis=0), 0.0)
            shift *= 2

        @pl.when(i == 0)
        def _():
            carry_ref[...] = jnp.zeros_like(carry_ref)

        c = c + carry_ref[...]
        carry_ref[...] = c[T - 1:T, :]
        lane = lax.broadcasted_iota(jnp.int32, c.shape, 1)
        for h_idx in range(HEADS):
            col = jnp.sum(jnp.where(lane == h_idx, c, 0.0), axis=-1, keepdims=True)
            rest = jnp.broadcast_to(col * (-LOG2E), (T, LANES))
            blk = jnp.zeros((T, LANES), F32)
            for t in range(BIAS_TERMS):
                term = rest.astype(BF16).astype(F32)
                blk = jnp.where(lane == t, term, blk)
                rest = rest - term
            kf_ref[h_idx, :, HEAD_DIM:] = blk.astype(BF16)

    @pl.when(j < SEG_DQ)
    def _():
        y_ref[...] = jnp.dot(h_ref[...], wfox_ref[...], preferred_element_type=F32)

    @pl.when(j >= SEG_DQ)
    def _():
        y_ref[...] = jnp.dot(h_ref[...], wdiff_ref[...], preferred_element_type=F32)

    y = y_ref
    heads = [slice(h_idx * HEAD_DIM, (h_idx + 1) * HEAD_DIM) for h_idx in range(HEADS)]

    def diff_qk(yt, g):
        c = cos_ref[...]
        s = sin_ref[...]
        parts = []
        for p in range(2):
            yp = yt[p * DIFF_QK:(p + 1) * DIFF_QK, :]
            ms = jnp.mean(yp * yp, axis=0, keepdims=True)
            n = yp * lax.rsqrt(ms + EPS) * g
            x1 = n[0:ROPE_HALF, :]
            x2 = n[ROPE_HALF:2 * ROPE_HALF, :]
            parts += [x1 * c - x2 * s, x2 * c + x1 * s, n[2 * ROPE_HALF:, :]]
        return jnp.concatenate(parts, axis=0)

    @pl.when(j == SEG_FQ)
    def _():
        g = fqg_ref[...] * (HEAD_DIM ** -0.5 * LOG2E)
        for hs in heads:
            yt = y[:, hs].T
            ms = jnp.mean(yt * yt, axis=0, keepdims=True)
            pt_ref[hs, :] = (yt * lax.rsqrt(ms + EPS) * g).astype(BF16)

    @pl.when(j == SEG_DQ)
    def _():
        g = dqg_ref[...] * (DIFF_QK ** -0.5 * LOG2E)
        for hs in heads:
            pt_ref[hs, :] = diff_qk(y[:, hs].T, g).astype(BF16)

    @pl.when((j == SEG_FV) | (j == SEG_DV))
    def _():
        for hs in heads:
            pt_ref[hs, :] = y[:, hs].T.astype(BF16)

    @pl.when(j == SEG_FK)
    def _():
        g = fkg_ref[...]
        for h_idx, hs in enumerate(heads):
            kf_ref[h_idx, :, :HEAD_DIM] = _rms_rows(y[:, hs], g).astype(BF16)

    @pl.when(j == SEG_DK)
    def _():
        g = dkg_ref[...]
        for h_idx, hs in enumerate(heads):
            kd_ref[h_idx] = diff_qk(y[:, hs].T, g).T.astype(BF16)


def _pt_slot(j):
    return ((j >= SEG_FV).astype(jnp.int32) + (j >= SEG_DQ).astype(jnp.int32)
            + (j >= SEG_DV).astype(jnp.int32))


def _proj(x, g1, w_fox, w_diff, wfg, bfg, fqg, fkg, dqg, dkg, cos_t, sin_t):
    col_tab = pl.BlockSpec((ROPE_HALF, T), lambda i, j: (0, i))
    whole = lambda r, c: pl.BlockSpec((r, c), lambda i, j: (0, 0))
    return pl.pallas_call(
        _proj_kernel,
        grid=(NT, 6),
        in_specs=[
            pl.BlockSpec((T, D_MODEL), lambda i, j: (i, 0)),
            whole(1, D_MODEL),
            pl.BlockSpec((D_MODEL, GROUP_W), lambda i, j: (0, jnp.minimum(j, SEG_FV))),
            pl.BlockSpec((D_MODEL, GROUP_W), lambda i, j: (0, jnp.maximum(j - SEG_DQ, 0))),
            whole(D_MODEL, LANES), whole(1, LANES),
            whole(HEAD_DIM, 1), whole(1, HEAD_DIM), whole(DIFF_QK, 1), whole(DIFF_QK, 1),
            col_tab, col_tab,
        ],
        out_specs=[
            pl.BlockSpec((None, None, GROUP_W, T), lambda i, j: (_pt_slot(j), i, 0, 0)),
            pl.BlockSpec((HEADS, T, 2 * HEAD_DIM), lambda i, j: (0, i, 0)),
            pl.BlockSpec((HEADS, T, HEAD_DIM), lambda i, j: (0, i, 0)),
        ],
        out_shape=[
            jax.ShapeDtypeStruct((4, NT, GROUP_W, T), BF16),
            jax.ShapeDtypeStruct((HEADS, SEQ, 2 * HEAD_DIM), BF16),
            jax.ShapeDtypeStruct((HEADS, SEQ, HEAD_DIM), BF16),
        ],
        scratch_shapes=[pltpu.VMEM((T, D_MODEL), BF16), pltpu.VMEM((1, LANES), F32),
                        pltpu.VMEM((T, GROUP_W), F32)],
        compiler_params=pltpu.CompilerParams(
            dimension_semantics=("arbitrary", "arbitrary"), vmem_limit_bytes=VMEM_LIMIT),
        name="proj",
    )(x, g1, w_fox, w_diff, wfg, bfg, fqg, fkg, dqg, dkg, cos_t, sin_t)


def _attend(i, scores, allowed, hi_cols, vt_ref, s_refs, cmax_refs, m_ref, acc_ref):
    m_ref[...] = jnp.full_like(m_ref, NEG)
    acc_ref[...] = jnp.zeros_like(acc_ref)

    def vt_ones(kv):
        return jnp.concatenate([vt_ref[kv], jnp.ones((SUM_ROWS, T), BF16)], axis=0)

    def produce(b, kv):
        s = scores(kv, None)
        s_refs[b][...] = s
        cmax_refs[b][...] = jnp.max(s, axis=0, keepdims=True)

    def consume(b, kv):
        m_old = m_ref[...]
        m_new = jnp.maximum(m_old, cmax_refs[b][...])
        p = jnp.exp2(s_refs[b][...] - m_new)
        alpha = jnp.exp2(m_old - m_new)
        acc_ref[...] = alpha * acc_ref[...] + jnp.dot(vt_ones(kv), p.astype(BF16), preferred_element_type=F32)
        m_ref[...] = m_new

    def consume_cols(s, kv, cols):
        load = lambda ref: jnp.concatenate([ref[:, c] for c in cols], axis=1)
        m_old = load(m_ref)
        m_new = jnp.maximum(m_old, jnp.max(s, axis=0, keepdims=True))
        p = jnp.exp2(s - m_new)
        alpha = jnp.exp2(m_old - m_new)
        acc_new = alpha * load(acc_ref) + jnp.dot(vt_ones(kv), p.astype(BF16), preferred_element_type=F32)
        off = 0
        for c in cols:
            part = slice(off, off + c.stop - c.start)
            m_ref[:, c], acc_ref[:, c] = m_new[:, part], acc_new[:, part]
            off = part.stop

    produce(0, 0)

    def body(a, carry):
        produce(1, 2 * a + 1)
        consume(0, 2 * a)
        produce(0, 2 * a + 2)
        consume(1, 2 * a + 1)
        return carry

    lax.fori_loop(0, i, body, 0)
    w_hi = sum(c.stop - c.start for c in hi_cols)
    s_refs[1][:, :w_hi] = jnp.where(allowed(0, hi_cols), scores(2 * i + 1, hi_cols), NEG)
    s_diag = jnp.where(allowed(0, None), s_refs[0][...], NEG)
    s_refs[0][...] = s_diag
    cmax_refs[0][...] = jnp.max(s_diag, axis=0, keepdims=True)
    consume(0, 2 * i)
    consume_cols(s_refs[1][:, :w_hi], 2 * i + 1, hi_cols)


def _fox_kernel(q_ref, k_ref, vt_ref, o_ref, sa_ref, sb_ref, ca_ref, cb_ref, m_ref, acc_ref):
    def with_bias_rows(q):
        row = lax.broadcasted_iota(jnp.int32, q.shape, 0)
        ones = jnp.where(row < BIAS_TERMS, 1.0, 0.0).astype(BF16)
        return jnp.concatenate([q, ones], axis=0)

    def allowed(key_off, cols):
        width = T if cols else TQ
        key = lax.broadcasted_iota(jnp.int32, (T, width), 0) + key_off
        qry = lax.broadcasted_iota(jnp.int32, (T, width), 1)
        return key <= qry

    hi_cols = [slice(T, TQ)]
    for sub in range(Q_PER_STEP):
        i = pl.program_id(1) * Q_PER_STEP + sub
        qb = with_bias_rows(jnp.concatenate([q_ref[2 * sub], q_ref[2 * sub + 1]], axis=1))

        def scores(kv, cols, sub=sub, qb=qb):
            off = pl.multiple_of(kv * T, T)
            rhs = with_bias_rows(q_ref[2 * sub + 1]) if cols else qb
            return jnp.dot(k_ref[pl.ds(off, T), :], rhs, preferred_element_type=F32)

        _attend(i, scores, allowed, hi_cols, vt_ref, (sa_ref, sb_ref), (ca_ref, cb_ref), m_ref, acc_ref)
        o = acc_ref[:HEAD_DIM, :] / acc_ref[HEAD_DIM:HEAD_DIM + 1, :]
        o_ref[sub * TQ:(sub + 1) * TQ, :] = o.T.astype(BF16)


def _diff_kernel(q_ref, k_ref, vt_ref, lq1_ref, lk1_ref, lq2_ref, lk2_ref, sg_ref,
                 o_ref, sa_ref, sb_ref, ca_ref, cb_ref, m_ref, acc_ref):
    def split_parts(q):
        part = lax.broadcasted_iota(jnp.int32, q.shape, 0) < DIFF_QK
        zero = jnp.zeros_like(q)
        return jnp.concatenate([jnp.where(part, q, zero), jnp.where(part, zero, q)], axis=1)

    def allowed(key_off, cols):
        width = T if cols else TQ
        key = lax.broadcasted_iota(jnp.int32, (T, 2 * width), 0) + key_off
        qry = lax.broadcasted_iota(jnp.int32, (T, 2 * width), 1) & (width - 1)
        return (key >> CHUNK_SHIFT) <= (qry >> CHUNK_SHIFT)

    hi_cols = [slice(T, TQ), slice(TQ + T, 2 * TQ)]
    lam = (jnp.exp(jnp.sum(lq1_ref[...] * lk1_ref[...], axis=-1, keepdims=True))
           - jnp.exp(jnp.sum(lq2_ref[...] * lk2_ref[...], axis=-1, keepdims=True))
           + LAMBDA_INIT)
    for sub in range(Q_PER_STEP):
        i = pl.program_id(1) * Q_PER_STEP + sub
        qq = split_parts(jnp.concatenate([q_ref[2 * sub], q_ref[2 * sub + 1]], axis=1))

        def scores(kv, cols, sub=sub, qq=qq):
            off = pl.multiple_of(kv * T, T)
            rhs = split_parts(q_ref[2 * sub + 1]) if cols else qq
            return jnp.dot(k_ref[pl.ds(off, T), :], rhs, preferred_element_type=F32)

        _attend(i, scores, allowed, hi_cols, vt_ref, (sa_ref, sb_ref), (ca_ref, cb_ref), m_ref, acc_ref)
        o_all = acc_ref[:HEAD_DIM, :] / acc_ref[HEAD_DIM:HEAD_DIM + 1, :]
        o = o_all[:, :TQ] - lam * o_all[:, TQ:]
        ms = jnp.mean(o * o, axis=0, keepdims=True)
        o = o * lax.rsqrt(ms + EPS) * (sg_ref[...] * (1.0 - LAMBDA_INIT))
        o_ref[sub * TQ:(sub + 1) * TQ, :] = o.T.astype(BF16)


def _attn_specs(q_seg, v_seg, k_width):
    return [
        pl.BlockSpec((None, 2 * Q_PER_STEP, HEAD_DIM, T), lambda h, i: (q_seg, i, h, 0)),
        pl.BlockSpec((None, SEQ, k_width), lambda h, i: (h, 0, 0)),
        pl.BlockSpec((None, NT, HEAD_DIM, T), lambda h, i: (v_seg, 0, h, 0)),
    ]


def _attn_scratch(width):
    return [pltpu.VMEM((T, width), F32), pltpu.VMEM((T, width), F32),
            pltpu.VMEM((1, width), F32), pltpu.VMEM((1, width), F32),
            pltpu.VMEM((1, width), F32), pltpu.VMEM((HEAD_DIM + SUM_ROWS, width), F32)]


def _fox_attn(pt, k_fox):
    return pl.pallas_call(
        _fox_kernel,
        grid=(HEADS, NQ // Q_PER_STEP),
        in_specs=_attn_specs(PT_FQ, PT_FV, 2 * HEAD_DIM),
        out_specs=pl.BlockSpec((Q_PER_STEP * TQ, HEAD_DIM), lambda h, i: (i, h)),
        out_shape=jax.ShapeDtypeStruct((SEQ, GROUP_W), BF16),
        scratch_shapes=_attn_scratch(TQ),
        compiler_params=pltpu.CompilerParams(
            dimension_semantics=("arbitrary", "arbitrary"), vmem_limit_bytes=VMEM_LIMIT),
        name="fox_attn",
    )(pt, k_fox, pt)


def _diff_attn(pt, k_diff, lq1, lk1, lq2, lk2, sg):
    lam_vec = pl.BlockSpec((1, DIFF_QK), lambda h, i: (0, 0))
    return pl.pallas_call(
        _diff_kernel,
        grid=(HEADS, NQ // Q_PER_STEP),
        in_specs=_attn_specs(PT_DQ, PT_DV, HEAD_DIM) + [lam_vec] * 4 + [pl.BlockSpec((HEAD_DIM, 1), lambda h, i: (0, 0))],
        out_specs=pl.BlockSpec((Q_PER_STEP * TQ, HEAD_DIM), lambda h, i: (i, h)),
        out_shape=jax.ShapeDtypeStruct((SEQ, GROUP_W), BF16),
        scratch_shapes=_attn_scratch(2 * TQ),
        compiler_params=pltpu.CompilerParams(
            dimension_semantics=("arbitrary", "arbitrary"), vmem_limit_bytes=VMEM_LIMIT),
        name="diff_attn",
    )(pt, k_diff, pt, lq1, lk1, lq2, lk2, sg)


def _outproj_kernel(of_ref, od_ref, wo_ref, x_ref, g2_ref, x1_ref, h2_ref):
    mix = (jnp.dot(of_ref[...], wo_ref[0:GROUP_W, :], preferred_element_type=F32)
           + jnp.dot(od_ref[...], wo_ref[GROUP_W:, :], preferred_element_type=F32))
    x1 = x_ref[...] + mix
    x1_ref[...] = x1
    h2_ref[...] = _rms_rows(x1, g2_ref[...]).astype(BF16)


def _outproj(o_f, o_d, wo, x, g2):
    return pl.pallas_call(
        _outproj_kernel,
        grid=(NT,),
        in_specs=[
            pl.BlockSpec((T, GROUP_W), lambda i: (i, 0)),
            pl.BlockSpec((T, GROUP_W), lambda i: (i, 0)),
            pl.BlockSpec((2 * GROUP_W, D_MODEL), lambda i: (0, 0)),
            pl.BlockSpec((T, D_MODEL), lambda i: (i, 0)),
            pl.BlockSpec((1, D_MODEL), lambda i: (0, 0)),
        ],
        out_specs=[pl.BlockSpec((T, D_MODEL), lambda i: (i, 0)),
                   pl.BlockSpec((T, D_MODEL), lambda i: (i, 0))],
        out_shape=[jax.ShapeDtypeStruct((SEQ, D_MODEL), F32),
                   jax.ShapeDtypeStruct((SEQ, D_MODEL), BF16)],
        compiler_params=pltpu.CompilerParams(
            dimension_semantics=("arbitrary",), vmem_limit_bytes=VMEM_LIMIT),
        name="outproj",
    )(o_f, o_d, wo, x, g2)


def _ffn_kernel(h2_ref, halo_ref, wg_ref, wv_ref, cwg_ref, cwv_ref, cbg_ref, cbv_ref, wd_ref, x1_ref,
                out_ref, he_ref, acc_ref, ug_ref, uv_ref):
    i = pl.program_id(0)
    j = pl.program_id(1)

    @pl.when(j == 0)
    def _():
        @pl.when(i == 0)
        def _():
            he_ref[0:HALO, :] = jnp.zeros((HALO, D_MODEL), BF16)

        @pl.when(i > 0)
        def _():
            he_ref[0:HALO, :] = halo_ref[...]

        he_ref[HALO:, :] = h2_ref[...]
        acc_ref[...] = jnp.zeros_like(acc_ref)

    he = he_ref[...]

    def conv(w_ref, cw_ref, cb_ref, u_ref):
        u_ref[...] = jnp.dot(he, w_ref[...], preferred_element_type=F32)
        cw = cw_ref[...]
        u_t = u_ref[HALO:, :]
        u_t1 = u_ref[HALO - 1:HALO - 1 + T, :]
        u_t2 = u_ref[HALO - 2:HALO - 2 + T, :]
        return cw[2:3, :] * u_t + cw[1:2, :] * u_t1 + cw[0:1, :] * u_t2 + cb_ref[...]

    gate = conv(wg_ref, cwg_ref, cbg_ref, ug_ref)
    val = conv(wv_ref, cwv_ref, cbv_ref, uv_ref)
    n_valid = D_FF - j * TF
    act = gate * jax.nn.sigmoid(gate) * val
    act = jnp.where(lax.broadcasted_iota(jnp.int32, act.shape, 1) < n_valid, act, 0.0).astype(BF16)
    wd = wd_ref[...]
    wd = jnp.where(lax.broadcasted_iota(jnp.int32, wd.shape, 0) < n_valid, wd, jnp.zeros_like(wd))
    acc_ref[...] += jnp.dot(act, wd, preferred_element_type=F32)

    @pl.when(j == NF - 1)
    def _():
        out_ref[...] = x1_ref[...] + acc_ref[...]


def _ffn(h2, w_gate, w_val, cw_gate, cw_val, cb_gate, cb_val, wd, x1):
    halo_blocks_per_tile = T // HALO
    return pl.pallas_call(
        _ffn_kernel,
        grid=(NT, NF),
        in_specs=[
            pl.BlockSpec((T, D_MODEL), lambda i, j: (i, 0)),
            pl.BlockSpec((HALO, D_MODEL), lambda i, j: (jnp.maximum(i * halo_blocks_per_tile - 1, 0), 0)),
            pl.BlockSpec((D_MODEL, TF), lambda i, j: (0, j)),
            pl.BlockSpec((D_MODEL, TF), lambda i, j: (0, j)),
            pl.BlockSpec((3, TF), lambda i, j: (0, j)),
            pl.BlockSpec((3, TF), lambda i, j: (0, j)),
            pl.BlockSpec((1, TF), lambda i, j: (0, j)),
            pl.BlockSpec((1, TF), lambda i, j: (0, j)),
            pl.BlockSpec((TF, D_MODEL), lambda i, j: (j, 0)),
            pl.BlockSpec((T, D_MODEL), lambda i, j: (i, 0)),
        ],
        out_specs=pl.BlockSpec((T, D_MODEL), lambda i, j: (i, 0)),
        out_shape=jax.ShapeDtypeStruct((SEQ, D_MODEL), F32),
        scratch_shapes=[pltpu.VMEM((HALO + T, D_MODEL), BF16), pltpu.VMEM((T, D_MODEL), F32),
                        pltpu.VMEM((HALO + T, TF), F32), pltpu.VMEM((HALO + T, TF), F32)],
        compiler_params=pltpu.CompilerParams(
            dimension_semantics=("arbitrary", "arbitrary"), vmem_limit_bytes=VMEM_LIMIT),
        name="ffn",
    )(h2, h2, w_gate, w_val, cw_gate, cw_val, cb_gate, cb_val, wd, x1)


def _rope_tables():
    inv_freq = ROPE_THETA ** (-jnp.arange(0, 2 * ROPE_HALF, 2, dtype=F32) / (2 * ROPE_HALF))
    ang = inv_freq[:, None] * jnp.arange(SEQ, dtype=F32)[None, :]
    return jnp.cos(ang), jnp.sin(ang)


def kernel(x, norm1_g, w_in, b_forget, fox_q_g, fox_k_g, diff_q_g, diff_k_g, lam_q1, lam_k1, lam_q2,
           lam_k2, diff_subln_g, w_out, norm2_g, w_up, conv_w, conv_b, w_down):
    assert x.shape == (1, SEQ, D_MODEL) and w_in.shape[0] == 1
    xs = x[0]
    w = w_in[0]
    n_fox = 3 * GROUP_W
    w_fox = w[:, :n_fox].astype(BF16)
    w_diff = w[:, n_fox + HEADS:].astype(BF16)
    wfg = jnp.pad(w[:, n_fox:n_fox + HEADS], ((0, 0), (0, LANES - HEADS))).astype(BF16)
    bfg = jnp.pad(b_forget, ((0, 0), (0, LANES - HEADS)))
    cos_t, sin_t = _rope_tables()

    pt, k_fox, k_diff = _proj(xs, norm1_g, w_fox, w_diff, wfg, bfg, fox_q_g[0][:, None], fox_k_g,
                              diff_q_g[0][:, None], diff_k_g[0][:, None], cos_t, sin_t)
    o_f = _fox_attn(pt, k_fox)
    o_d = _diff_attn(pt, k_diff, lam_q1, lam_k1, lam_q2, lam_k2, diff_subln_g[0][:, None])
    x1, h2 = _outproj(o_f, o_d, w_out[0].astype(BF16), xs, norm2_g)

    wu = w_up[0]
    cw = conv_w[0]
    out = _ffn(h2, wu[:, :D_FF].astype(BF16), wu[:, D_FF:].astype(BF16), cw[:, :D_FF], cw[:, D_FF:],
               conv_b[:, :D_FF], conv_b[:, D_FF:], w_down[0].astype(BF16), x1)
    return out[None]
```

```python
import math

import jax
import jax.numpy as jnp
from jax import lax
from jax.experimental import pallas as pl
from jax.experimental.pallas import tpu as pltpu

D_MODEL = 2048
SEQ = 8192
HEADS = 8
HEAD_DIM = 128
DIFF_QK = 64
ROPE_HALF = 8
ROPE_THETA = 500000.0
GROUP_W = HEADS * HEAD_DIM
D_FF = 5504
EPS = 1e-6
NEG = -1e30
LOG2E = 1.4426950408889634
LAMBDA_INIT = 0.8 - 0.6 * math.exp(-0.3 * 0)
CHUNK_SHIFT = 6

LANES = 128
T = 512
NT = SEQ // T
TF = 512
NF = -(-D_FF // TF)
TQ = 2 * T
NQ = SEQ // TQ
Q_PER_STEP = 2
SUM_ROWS = 16
ATTN_STEPS = HEADS * (NQ // Q_PER_STEP)
BIAS_TERMS = 3
HALO = 16
VMEM_LIMIT = 56 * 1024 * 1024

F32 = jnp.float32
BF16 = jnp.bfloat16


def _rms_rows(x, g):
    ms = jnp.mean(x * x, axis=-1, keepdims=True)
    return x * lax.rsqrt(ms + EPS) * g


SEG_FQ, SEG_FK, SEG_FV, SEG_DQ, SEG_DK, SEG_DV = range(6)
PT_FQ, PT_FV, PT_DQ, PT_DV = range(4)


def _proj_kernel(x_ref, g1_ref, wfox_ref, wdiff_ref, wfg_ref, bf_ref, fqg_ref, fkg_ref, dqg_ref, dkg_ref,
                 cos_ref, sin_ref, pt_ref, kf_ref, kd_ref, h_ref, carry_ref, y_ref):
    i = pl.program_id(0)
    j = pl.program_id(1)

    @pl.when(j == 0)
    def _():
        h = _rms_rows(x_ref[...], g1_ref[...]).astype(BF16)
        h_ref[...] = h
        z = jnp.dot(h, wfg_ref[...], preferred_element_type=F32) + bf_ref[...]
        c = jnp.minimum(z, 0.0) - jnp.log1p(jnp.exp(-jnp.abs(z)))
        row = lax.broadcasted_iota(jnp.int32, c.shape, 0)
        shift = 1
        while shift < T:
            c = c + jnp.where(row >= shift, pltpu.roll(c, shift, axis=0), 0.0)
            shift *= 2

        @pl.when(i == 0)
        def _():
            carry_ref[...] = jnp.zeros_like(carry_ref)

        c = c + carry_ref[...]
        carry_ref[...] = c[T - 1:T, :]
        lane = lax.broadcasted_iota(jnp.int32, c.shape, 1)
        for h_idx in range(HEADS):
            col = jnp.sum(jnp.where(lane == h_idx, c, 0.0), axis=-1, keepdims=True)
            rest = jnp.broadcast_to(col * (-LOG2E), (T, LANES))
            blk = jnp.zeros((T, LANES), F32)
            for t in range(BIAS_TERMS):
                term = rest.astype(BF16).astype(F32)
                blk = jnp.where(lane == t, term, blk)
                rest = rest - term
            kf_ref[h_idx, :, HEAD_DIM:] = blk.astype(BF16)

    @pl.when(j < SEG_DQ)
    def _():
        y_ref[...] = jnp.dot(h_ref[...], wfox_ref[...], preferred_element_type=F32)

    @pl.when(j >= SEG_DQ)
    def _():
        y_ref[...] = jnp.dot(h_ref[...], wdiff_ref[...], preferred_element_type=F32)

    y = y_ref
    heads = [slice(h_idx * HEAD_DIM, (h_idx + 1) * HEAD_DIM) for h_idx in range(HEADS)]

    def diff_qk(yt, g):
        c = cos_ref[...]
        s = sin_ref[...]
        parts = []
        for p in range(2):
            yp = yt[p * DIFF_QK:(p + 1) * DIFF_QK, :]
            ms = jnp.mean(yp * yp, axis=0, keepdims=True)
            n = yp * lax.rsqrt(ms + EPS) * g
            x1 = n[0:ROPE_HALF, :]
            x2 = n[ROPE_HALF:2 * ROPE_HALF, :]
            parts += [x1 * c - x2 * s, x2 * c + x1 * s, n[2 * ROPE_HALF:, :]]
        return jnp.concatenate(parts, axis=0)

    @pl.when(j == SEG_FQ)
    def _():
        g = fqg_ref[...] * (HEAD_DIM ** -0.5 * LOG2E)
        for hs in heads:
            yt = y[:, hs].T
            ms = jnp.mean(yt * yt, axis=0, keepdims=True)
            pt_ref[hs, :] = (yt * lax.rsqrt(ms + EPS) * g).astype(BF16)

    @pl.when(j == SEG_DQ)
    def _():
        g = dqg_ref[...] * (DIFF_QK ** -0.5 * LOG2E)
        for hs in heads:
            pt_ref[hs, :] = diff_qk(y[:, hs].T, g).astype(BF16)

    @pl.when((j == SEG_FV) | (j == SEG_DV))
    def _():
        for hs in heads:
            pt_ref[hs, :] = y[:, hs].T.astype(BF16)

    @pl.when(j == SEG_FK)
    def _():
        g = fkg_ref[...]
        for h_idx, hs in enumerate(heads):
            kf_ref[h_idx, :, :HEAD_DIM] = _rms_rows(y[:, hs], g).astype(BF16)

    @pl.when(j == SEG_DK)
    def _():
        g = dkg_ref[...]
        for h_idx, hs in enumerate(heads):
            kd_ref[h_idx] = diff_qk(y[:, hs].T, g).T.astype(BF16)


def _pt_slot(j):
    return ((j >= SEG_FV).astype(jnp.int32) + (j >= SEG_DQ).astype(jnp.int32)
            + (j >= SEG_DV).astype(jnp.int32))


def _proj(x, g1, w_fox, w_diff, wfg, bfg, fqg, fkg, dqg, dkg, cos_t, sin_t):
    col_tab = pl.BlockSpec((ROPE_HALF, T), lambda i, j: (0, i))
    whole = lambda r, c: pl.BlockSpec((r, c), lambda i, j: (0, 0))
    return pl.pallas_call(
        _proj_kernel,
        grid=(NT, 6),
        in_specs=[
            pl.BlockSpec((T, D_MODEL), lambda i, j: (i, 0)),
            whole(1, D_MODEL),
            pl.BlockSpec((D_MODEL, GROUP_W), lambda i, j: (0, jnp.minimum(j, SEG_FV))),
            pl.BlockSpec((D_MODEL, GROUP_W), lambda i, j: (0, jnp.maximum(j - SEG_DQ, 0))),
            whole(D_MODEL, LANES), whole(1, LANES),
            whole(HEAD_DIM, 1), whole(1, HEAD_DIM), whole(DIFF_QK, 1), whole(DIFF_QK, 1),
            col_tab, col_tab,
        ],
        out_specs=[
            pl.BlockSpec((None, None, GROUP_W, T), lambda i, j: (_pt_slot(j), i, 0, 0)),
            pl.BlockSpec((HEADS, T, 2 * HEAD_DIM), lambda i, j: (0, i, 0)),
            pl.BlockSpec((HEADS, T, HEAD_DIM), lambda i, j: (0, i, 0)),
        ],
        out_shape=[
            jax.ShapeDtypeStruct((4, NT, GROUP_W, T), BF16),
            jax.ShapeDtypeStruct((HEADS, SEQ, 2 * HEAD_DIM), BF16),
            jax.ShapeDtypeStruct((HEADS, SEQ, HEAD_DIM), BF16),
        ],
        scratch_shapes=[pltpu.VMEM((T, D_MODEL), BF16), pltpu.VMEM((1, LANES), F32),
                        pltpu.VMEM((T, GROUP_W), F32)],
        compiler_params=pltpu.CompilerParams(
            dimension_semantics=("arbitrary", "arbitrary"), vmem_limit_bytes=VMEM_LIMIT),
        name="proj",
    )(x, g1, w_fox, w_diff, wfg, bfg, fqg, fkg, dqg, dkg, cos_t, sin_t)


def _attend(i, scores, allowed, hi_cols, vt_ref, s_refs, cmax_refs, m_ref, acc_ref):
    m_ref[...] = jnp.full_like(m_ref, NEG)
    acc_ref[...] = jnp.zeros_like(acc_ref)

    def vt_ones(kv):
        return jnp.concatenate([vt_ref[kv], jnp.ones((SUM_ROWS, T), BF16)], axis=0)

    def produce(b, kv):
        s = scores(kv, None)
        s_refs[b][...] = s
        cmax_refs[b][...] = jnp.max(s, axis=0, keepdims=True)

    def consume(b, kv):
        m_old = m_ref[...]
        m_new = jnp.maximum(m_old, cmax_refs[b][...])
        p = jnp.exp2(s_refs[b][...] - m_new)
        alpha = jnp.exp2(m_old - m_new)
        acc_ref[...] = alpha * acc_ref[...] + jnp.dot(vt_ones(kv), p.astype(BF16), preferred_element_type=F32)
        m_ref[...] = m_new

    def consume_cols(s, kv, cols):
        load = lambda ref: jnp.concatenate([ref[:, c] for c in cols], axis=1)
        m_old = load(m_ref)
        m_new = jnp.maximum(m_old, jnp.max(s, axis=0, keepdims=True))
        p = jnp.exp2(s - m_new)
        alpha = jnp.exp2(m_old - m_new)
        acc_new = alpha * load(acc_ref) + jnp.dot(vt_ones(kv), p.astype(BF16), preferred_element_type=F32)
        off = 0
        for c in cols:
            part = slice(off, off + c.stop - c.start)
            m_ref[:, c], acc_ref[:, c] = m_new[:, part], acc_new[:, part]
            off = part.stop

    produce(0, 0)

    def body(a, carry):
        produce(1, 2 * a + 1)
        consume(0, 2 * a)
        produce(0, 2 * a + 2)
        consume(1, 2 * a + 1)
        return carry

    lax.fori_loop(0, i, body, 0)
    w_hi = sum(c.stop - c.start for c in hi_cols)
    s_refs[1][:, :w_hi] = jnp.where(allowed(0, hi_cols), scores(2 * i + 1, hi_cols), NEG)
    s_diag = jnp.where(allowed(0, None), s_refs[0][...], NEG)
    s_refs[0][...] = s_diag
    cmax_refs[0][...] = jnp.max(s_diag, axis=0, keepdims=True)
    consume(0, 2 * i)
    consume_cols(s_refs[1][:, :w_hi], 2 * i + 1, hi_cols)


def _fox_kernel(q_ref, k_ref, vt_ref, wdown_ref, wout_ref, o_ref, wdown_bf_ref, wout_bf_ref,
                sa_ref, sb_ref, ca_ref, cb_ref, m_ref, acc_ref):
    wdown_bf_ref[...] = wdown_ref[...].astype(BF16)
    wout_bf_ref[...] = wout_ref[...].astype(BF16)

    def with_bias_rows(q):
        row = lax.broadcasted_iota(jnp.int32, q.shape, 0)
        ones = jnp.where(row < BIAS_TERMS, 1.0, 0.0).astype(BF16)
        return jnp.concatenate([q, ones], axis=0)

    def allowed(key_off, cols):
        width = T if cols else TQ
        key = lax.broadcasted_iota(jnp.int32, (T, width), 0) + key_off
        qry = lax.broadcasted_iota(jnp.int32, (T, width), 1)
        return key <= qry

    hi_cols = [slice(T, TQ)]
    for sub in range(Q_PER_STEP):
        i = pl.program_id(1) * Q_PER_STEP + sub
        qb = with_bias_rows(jnp.concatenate([q_ref[2 * sub], q_ref[2 * sub + 1]], axis=1))

        def scores(kv, cols, sub=sub, qb=qb):
            off = pl.multiple_of(kv * T, T)
            rhs = with_bias_rows(q_ref[2 * sub + 1]) if cols else qb
            return jnp.dot(k_ref[pl.ds(off, T), :], rhs, preferred_element_type=F32)

        _attend(i, scores, allowed, hi_cols, vt_ref, (sa_ref, sb_ref), (ca_ref, cb_ref), m_ref, acc_ref)
        o = acc_ref[:HEAD_DIM, :] / acc_ref[HEAD_DIM:HEAD_DIM + 1, :]
        o_ref[sub * TQ:(sub + 1) * TQ, :] = o.T.astype(BF16)


def _diff_kernel(q_ref, k_ref, vt_ref, lq1_ref, lk1_ref, lq2_ref, lk2_ref, sg_ref,
                 wup_ref, o_ref, wgate_ref, wval_ref, sa_ref, sb_ref, ca_ref, cb_ref, m_ref, acc_ref):
    wgate_ref[...] = wup_ref[:, :D_FF].astype(BF16)
    wval_ref[...] = wup_ref[:, D_FF:].astype(BF16)

    def split_parts(q):
        part = lax.broadcasted_iota(jnp.int32, q.shape, 0) < DIFF_QK
        zero = jnp.zeros_like(q)
        return jnp.concatenate([jnp.where(part, q, zero), jnp.where(part, zero, q)], axis=1)

    def allowed(key_off, cols):
        width = T if cols else TQ
        key = lax.broadcasted_iota(jnp.int32, (T, 2 * width), 0) + key_off
        qry = lax.broadcasted_iota(jnp.int32, (T, 2 * width), 1) & (width - 1)
        return (key >> CHUNK_SHIFT) <= (qry >> CHUNK_SHIFT)

    hi_cols = [slice(T, TQ), slice(TQ + T, 2 * TQ)]
    lam = (jnp.exp(jnp.sum(lq1_ref[...] * lk1_ref[...], axis=-1, keepdims=True))
           - jnp.exp(jnp.sum(lq2_ref[...] * lk2_ref[...], axis=-1, keepdims=True))
           + LAMBDA_INIT)
    for sub in range(Q_PER_STEP):
        i = pl.program_id(1) * Q_PER_STEP + sub
        qq = split_parts(jnp.concatenate([q_ref[2 * sub], q_ref[2 * sub + 1]], axis=1))

        def scores(kv, cols, sub=sub, qq=qq):
            off = pl.multiple_of(kv * T, T)
            rhs = split_parts(q_ref[2 * sub + 1]) if cols else qq
            return jnp.dot(k_ref[pl.ds(off, T), :], rhs, preferred_element_type=F32)

        _attend(i, scores, allowed, hi_cols, vt_ref, (sa_ref, sb_ref), (ca_ref, cb_ref), m_ref, acc_ref)
        o_all = acc_ref[:HEAD_DIM, :] / acc_ref[HEAD_DIM:HEAD_DIM + 1, :]
        o = o_all[:, :TQ] - lam * o_all[:, TQ:]
        ms = jnp.mean(o * o, axis=0, keepdims=True)
        o = o * lax.rsqrt(ms + EPS) * (sg_ref[...] * (1.0 - LAMBDA_INIT))
        o_ref[sub * TQ:(sub + 1) * TQ, :] = o.T.astype(BF16)


def _attn_specs(q_seg, v_seg, k_width):
    return [
        pl.BlockSpec((None, 2 * Q_PER_STEP, HEAD_DIM, T), lambda h, i: (q_seg, i, h, 0)),
        pl.BlockSpec((None, SEQ, k_width), lambda h, i: (h, 0, 0)),
        pl.BlockSpec((None, NT, HEAD_DIM, T), lambda h, i: (v_seg, 0, h, 0)),
    ]


def _attn_scratch(width):
    return [pltpu.VMEM((T, width), F32), pltpu.VMEM((T, width), F32),
            pltpu.VMEM((1, width), F32), pltpu.VMEM((1, width), F32),
            pltpu.VMEM((1, width), F32), pltpu.VMEM((HEAD_DIM + SUM_ROWS, width), F32)]


def _slab_spec(rows, cols):
    return pl.BlockSpec((rows, cols), lambda h, i: (h * (NQ // Q_PER_STEP) + i, 0))


WROW_SLAB = D_MODEL // ATTN_STEPS
WDOWN_SLAB = 16 * -(-D_FF // (16 * ATTN_STEPS))


def _fox_attn(pt, k_fox, w_down, w_out):
    return pl.pallas_call(
        _fox_kernel,
        grid=(HEADS, NQ // Q_PER_STEP),
        in_specs=_attn_specs(PT_FQ, PT_FV, 2 * HEAD_DIM) + [_slab_spec(WDOWN_SLAB, D_MODEL),
                                                          _slab_spec(WROW_SLAB, D_MODEL)],
        out_specs=[pl.BlockSpec((Q_PER_STEP * TQ, HEAD_DIM), lambda h, i: (i, h)),
                   _slab_spec(WDOWN_SLAB, D_MODEL), _slab_spec(WROW_SLAB, D_MODEL)],
        out_shape=[jax.ShapeDtypeStruct((SEQ, GROUP_W), BF16),
                   jax.ShapeDtypeStruct((D_FF, D_MODEL), BF16),
                   jax.ShapeDtypeStruct((2 * GROUP_W, D_MODEL), BF16)],
        scratch_shapes=_attn_scratch(TQ),
        compiler_params=pltpu.CompilerParams(
            dimension_semantics=("arbitrary", "arbitrary"), vmem_limit_bytes=VMEM_LIMIT),
        name="fox_attn",
    )(pt, k_fox, pt, w_down, w_out)


def _diff_attn(pt, k_diff, lq1, lk1, lq2, lk2, sg, w_up):
    lam_vec = pl.BlockSpec((1, DIFF_QK), lambda h, i: (0, 0))
    return pl.pallas_call(
        _diff_kernel,
        grid=(HEADS, NQ // Q_PER_STEP),
        in_specs=(_attn_specs(PT_DQ, PT_DV, HEAD_DIM) + [lam_vec] * 4
                  + [pl.BlockSpec((HEAD_DIM, 1), lambda h, i: (0, 0)), _slab_spec(WROW_SLAB, 2 * D_FF)]),
        out_specs=[pl.BlockSpec((Q_PER_STEP * TQ, HEAD_DIM), lambda h, i: (i, h)),
                   _slab_spec(WROW_SLAB, D_FF), _slab_spec(WROW_SLAB, D_FF)],
        out_shape=[jax.ShapeDtypeStruct((SEQ, GROUP_W), BF16),
                   jax.ShapeDtypeStruct((D_MODEL, D_FF), BF16),
                   jax.ShapeDtypeStruct((D_MODEL, D_FF), BF16)],
        scratch_shapes=_attn_scratch(2 * TQ),
        compiler_params=pltpu.CompilerParams(
            dimension_semantics=("arbitrary", "arbitrary"), vmem_limit_bytes=VMEM_LIMIT),
        name="diff_attn",
    )(pt, k_diff, pt, lq1, lk1, lq2, lk2, sg, w_up)


def _outproj_kernel(of_ref, od_ref, wo_ref, x_ref, g2_ref, x1_ref, h2_ref):
    mix = (jnp.dot(of_ref[...], wo_ref[0:GROUP_W, :], preferred_element_type=F32)
           + jnp.dot(od_ref[...], wo_ref[GROUP_W:, :], preferred_element_type=F32))
    x1 = x_ref[...] + mix
    x1_ref[...] = x1
    h2_ref[...] = _rms_rows(x1, g2_ref[...]).astype(BF16)


def _outproj(o_f, o_d, wo, x, g2):
    return pl.pallas_call(
        _outproj_kernel,
        grid=(NT,),
        in_specs=[
            pl.BlockSpec((T, GROUP_W), lambda i: (i, 0)),
            pl.BlockSpec((T, GROUP_W), lambda i: (i, 0)),
            pl.BlockSpec((2 * GROUP_W, D_MODEL), lambda i: (0, 0)),
            pl.BlockSpec((T, D_MODEL), lambda i: (i, 0)),
            pl.BlockSpec((1, D_MODEL), lambda i: (0, 0)),
        ],
        out_specs=[pl.BlockSpec((T, D_MODEL), lambda i: (i, 0)),
                   pl.BlockSpec((T, D_MODEL), lambda i: (i, 0))],
        out_shape=[jax.ShapeDtypeStruct((SEQ, D_MODEL), F32),
                   jax.ShapeDtypeStruct((SEQ, D_MODEL), BF16)],
        compiler_params=pltpu.CompilerParams(
            dimension_semantics=("arbitrary",), vmem_limit_bytes=VMEM_LIMIT),
        name="outproj",
    )(o_f, o_d, wo, x, g2)


def _ffn_kernel(h2_ref, halo_ref, wg_ref, wv_ref, cwg_ref, cwv_ref, cbg_ref, cbv_ref, wd_ref, x1_ref,
                out_ref, he_ref, acc_ref, ug_ref, uv_ref):
    i = pl.program_id(0)
    j = pl.program_id(1)

    @pl.when(j == 0)
    def _():
        @pl.when(i == 0)
        def _():
            he_ref[0:HALO, :] = jnp.zeros((HALO, D_MODEL), BF16)

        @pl.when(i > 0)
        def _():
            he_ref[0:HALO, :] = halo_ref[...]

        he_ref[HALO:, :] = h2_ref[...]
        acc_ref[...] = jnp.zeros_like(acc_ref)

    he = he_ref[...]

    def conv(w_ref, cw_ref, cb_ref, u_ref):
        u_ref[...] = jnp.dot(he, w_ref[...], preferred_element_type=F32)
        cw = cw_ref[...]
        u_t = u_ref[HALO:, :]
        u_t1 = u_ref[HALO - 1:HALO - 1 + T, :]
        u_t2 = u_ref[HALO - 2:HALO - 2 + T, :]
        return cw[2:3, :] * u_t + cw[1:2, :] * u_t1 + cw[0:1, :] * u_t2 + cb_ref[...]

    gate = conv(wg_ref, cwg_ref, cbg_ref, ug_ref)
    val = conv(wv_ref, cwv_ref, cbv_ref, uv_ref)
    n_valid = D_FF - j * TF
    act = gate * jax.nn.sigmoid(gate) * val
    act = jnp.where(lax.broadcasted_iota(jnp.int32, act.shape, 1) < n_valid, act, 0.0).astype(BF16)
    wd = wd_ref[...]
    wd = jnp.where(lax.broadcasted_iota(jnp.int32, wd.shape, 0) < n_valid, wd, jnp.zeros_like(wd))
    acc_ref[...] += jnp.dot(act, wd, preferred_element_type=F32)

    @pl.when(j == NF - 1)
    def _():
        out_ref[...] = x1_ref[...] + acc_ref[...]


def _ffn(h2, w_gate, w_val, cw_gate, cw_val, cb_gate, cb_val, wd, x1):
    halo_blocks_per_tile = T // HALO
    return pl.pallas_call(
        _ffn_kernel,
        grid=(NT, NF),
        in_specs=[
            pl.BlockSpec((T, D_MODEL), lambda i, j: (i, 0)),
            pl.BlockSpec((HALO, D_MODEL), lambda i, j: (jnp.maximum(i * halo_blocks_per_tile - 1, 0), 0)),
            pl.BlockSpec((D_MODEL, TF), lambda i, j: (0, j)),
            pl.BlockSpec((D_MODEL, TF), lambda i, j: (0, j)),
            pl.BlockSpec((3, TF), lambda i, j: (0, j)),
            pl.BlockSpec((3, TF), lambda i, j: (0, j)),
            pl.BlockSpec((1, TF), lambda i, j: (0, j)),
            pl.BlockSpec((1, TF), lambda i, j: (0, j)),
            pl.BlockSpec((TF, D_MODEL), lambda i, j: (j, 0)),
            pl.BlockSpec((T, D_MODEL), lambda i, j: (i, 0)),
        ],
        out_specs=pl.BlockSpec((T, D_MODEL), lambda i, j: (i, 0)),
        out_shape=jax.ShapeDtypeStruct((SEQ, D_MODEL), F32),
        scratch_shapes=[pltpu.VMEM((HALO + T, D_MODEL), BF16), pltpu.VMEM((T, D_MODEL), F32),
                        pltpu.VMEM((HALO + T, TF), F32), pltpu.VMEM((HALO + T, TF), F32)],
        compiler_params=pltpu.CompilerParams(
            dimension_semantics=("arbitrary", "arbitrary"), vmem_limit_bytes=VMEM_LIMIT),
        name="ffn",
    )(h2, h2, w_gate, w_val, cw_gate, cw_val, cb_gate, cb_val, wd, x1)


def _rope_tables():
    inv_freq = ROPE_THETA ** (-jnp.arange(0, 2 * ROPE_HALF, 2, dtype=F32) / (2 * ROPE_HALF))
    ang = inv_freq[:, None] * jnp.arange(SEQ, dtype=F32)[None, :]
    return jnp.cos(ang), jnp.sin(ang)


def kernel(x, norm1_g, w_in, b_forget, fox_q_g, fox_k_g, diff_q_g, diff_k_g, lam_q1, lam_k1, lam_q2,
           lam_k2, diff_subln_g, w_out, norm2_g, w_up, conv_w, conv_b, w_down):
    assert x.shape == (1, SEQ, D_MODEL) and w_in.shape[0] == 1
    xs = x[0]
    w = w_in[0]
    n_fox = 3 * GROUP_W
    w_fox = w[:, :n_fox].astype(BF16)
    w_diff = w[:, n_fox + HEADS:].astype(BF16)
    wfg = jnp.pad(w[:, n_fox:n_fox + HEADS], ((0, 0), (0, LANES - HEADS))).astype(BF16)
    bfg = jnp.pad(b_forget, ((0, 0), (0, LANES - HEADS)))
    cos_t, sin_t = _rope_tables()

    pt, k_fox, k_diff = _proj(xs, norm1_g, w_fox, w_diff, wfg, bfg, fox_q_g[0][:, None], fox_k_g,
                              diff_q_g[0][:, None], diff_k_g[0][:, None], cos_t, sin_t)
    o_f, wd_bf, wo_bf = _fox_attn(pt, k_fox, w_down[0], w_out[0])
    o_d, w_gate, w_val = _diff_attn(pt, k_diff, lam_q1, lam_k1, lam_q2, lam_k2, diff_subln_g[0][:, None], w_up[0])
    x1, h2 = _outproj(o_f, o_d, wo_bf, xs, norm2_g)

    cw = conv_w[0]
    out = _ffn(h2, w_gate, w_val, cw[:, :D_FF], cw[:, D_FF:], conv_b[:, :D_FF], conv_b[:, D_FF:], wd_bf, x1)
    return out[None]
```

```python
import math

import jax
import jax.numpy as jnp
from jax import lax
from jax.experimental import pallas as pl
from jax.experimental.pallas import tpu as pltpu

D_MODEL = 2048
SEQ = 8192
HEADS = 8
HEAD_DIM = 128
DIFF_QK = 64
ROPE_HALF = 8
ROPE_THETA = 500000.0
GROUP_W = HEADS * HEAD_DIM
D_FF = 5504
EPS = 1e-6
NEG = -1e30
LOG2E = 1.4426950408889634
LAMBDA_INIT = 0.8 - 0.6 * math.exp(-0.3 * 0)
CHUNK_SHIFT = 6

LANES = 128
T = 512
NT = SEQ // T
TF = 512
NF = -(-D_FF // TF)
TQ = 2 * T
NQ = SEQ // TQ
BF16_ROWS = 16
SUM_ROWS = BF16_ROWS
FOX_Q_PER_STEP, FOX_PAIRS_PER_TRIP = 4, 2
DIFF_Q_PER_STEP, DIFF_PAIRS_PER_TRIP = 2, 1
BIAS_TERMS = 3
HALO = BF16_ROWS
VMEM_LIMIT = 56 * 1024 * 1024

F32 = jnp.float32
BF16 = jnp.bfloat16


def _rms_rows(x, g):
    ms = jnp.mean(x * x, axis=-1, keepdims=True)
    return x * lax.rsqrt(ms + EPS) * g


SEG_FQ, SEG_FK, SEG_FV, SEG_DQ, SEG_DK, SEG_DV = range(6)
PT_FQ, PT_FV, PT_DQ, PT_DV = range(4)


def _proj_kernel(x_ref, g1_ref, wfox_ref, wdiff_ref, wfg_ref, bf_ref, fqg_ref, fkg_ref, dqg_ref, dkg_ref,
                 cos_ref, sin_ref, pt_ref, kf_ref, kd_ref, h_ref, carry_ref, y_ref):
    i = pl.program_id(0)
    j = pl.program_id(1)

    @pl.when(j == 0)
    def _():
        h = _rms_rows(x_ref[...], g1_ref[...]).astype(BF16)
        h_ref[...] = h
        z = jnp.dot(h, wfg_ref[...], preferred_element_type=F32) + bf_ref[...]
        c = jnp.minimum(z, 0.0) - jnp.log1p(jnp.exp(-jnp.abs(z)))
        row = lax.broadcasted_iota(jnp.int32, c.shape, 0)
        shift = 1
        while shift < T:
            c = c + jnp.where(row >= shift, pltpu.roll(c, shift, axis=0), 0.0)
            shift *= 2

        @pl.when(i == 0)
        def _():
            carry_ref[...] = jnp.zeros_like(carry_ref)

        c = c + carry_ref[...]
        carry_ref[...] = c[T - 1:T, :]
        lane = lax.broadcasted_iota(jnp.int32, c.shape, 1)
        for h_idx in range(HEADS):
            col = jnp.sum(jnp.where(lane == h_idx, c, 0.0), axis=-1, keepdims=True)
            rest = jnp.broadcast_to(col * (-LOG2E), (T, LANES))
            blk = jnp.zeros((T, LANES), F32)
            for t in range(BIAS_TERMS):
                term = rest.astype(BF16).astype(F32)
                blk = jnp.where(lane == t, term, blk)
                rest = rest - term
            kf_ref[h_idx, :, HEAD_DIM:] = blk.astype(BF16)

    @pl.when(j < SEG_DQ)
    def _():
        y_ref[...] = jnp.dot(h_ref[...], wfox_ref[...], preferred_element_type=F32)

    @pl.when(j >= SEG_DQ)
    def _():
        y_ref[...] = jnp.dot(h_ref[...], wdiff_ref[...], preferred_element_type=F32)

    y = y_ref
    heads = [slice(h_idx * HEAD_DIM, (h_idx + 1) * HEAD_DIM) for h_idx in range(HEADS)]

    def diff_qk(yt, g):
        c = cos_ref[...]
        s = sin_ref[...]
        parts = []
        for p in range(2):
            yp = yt[p * DIFF_QK:(p + 1) * DIFF_QK, :]
            ms = jnp.mean(yp * yp, axis=0, keepdims=True)
            n = yp * lax.rsqrt(ms + EPS) * g
            x1 = n[0:ROPE_HALF, :]
            x2 = n[ROPE_HALF:2 * ROPE_HALF, :]
            parts += [x1 * c - x2 * s, x2 * c + x1 * s, n[2 * ROPE_HALF:, :]]
        return jnp.concatenate(parts, axis=0)

    @pl.when(j == SEG_FQ)
    def _():
        g = fqg_ref[...] * (HEAD_DIM ** -0.5 * LOG2E)
        for hs in heads:
            yt = y[:, hs].T
            ms = jnp.mean(yt * yt, axis=0, keepdims=True)
            pt_ref[hs, :] = (yt * lax.rsqrt(ms + EPS) * g).astype(BF16)

    @pl.when(j == SEG_DQ)
    def _():
        g = dqg_ref[...] * (DIFF_QK ** -0.5 * LOG2E)
        for hs in heads:
            pt_ref[hs, :] = diff_qk(y[:, hs].T, g).astype(BF16)

    @pl.when((j == SEG_FV) | (j == SEG_DV))
    def _():
        for hs in heads:
            pt_ref[hs, :] = y[:, hs].T.astype(BF16)

    @pl.when(j == SEG_FK)
    def _():
        g = fkg_ref[...]
        for h_idx, hs in enumerate(heads):
            kf_ref[h_idx, :, :HEAD_DIM] = _rms_rows(y[:, hs], g).astype(BF16)

    @pl.when(j == SEG_DK)
    def _():
        g = dkg_ref[...]
        for h_idx, hs in enumerate(heads):
            kd_ref[h_idx] = diff_qk(y[:, hs].T, g).T.astype(BF16)


def _pt_slot(j):
    return ((j >= SEG_FV).astype(jnp.int32) + (j >= SEG_DQ).astype(jnp.int32)
            + (j >= SEG_DV).astype(jnp.int32))


def _proj(x, g1, w_fox, w_diff, wfg, bfg, fqg, fkg, dqg, dkg, cos_t, sin_t):
    col_tab = pl.BlockSpec((ROPE_HALF, T), lambda i, j: (0, i))
    whole = lambda r, c: pl.BlockSpec((r, c), lambda i, j: (0, 0))
    return pl.pallas_call(
        _proj_kernel,
        grid=(NT, 6),
        in_specs=[
            pl.BlockSpec((T, D_MODEL), lambda i, j: (i, 0)),
            whole(1, D_MODEL),
            pl.BlockSpec((D_MODEL, GROUP_W), lambda i, j: (0, jnp.minimum(j, SEG_FV))),
            pl.BlockSpec((D_MODEL, GROUP_W), lambda i, j: (0, jnp.maximum(j - SEG_DQ, 0))),
            whole(D_MODEL, LANES), whole(1, LANES),
            whole(HEAD_DIM, 1), whole(1, HEAD_DIM), whole(DIFF_QK, 1), whole(DIFF_QK, 1),
            col_tab, col_tab,
        ],
        out_specs=[
            pl.BlockSpec((None, None, GROUP_W, T), lambda i, j: (_pt_slot(j), i, 0, 0)),
            pl.BlockSpec((HEADS, T, 2 * HEAD_DIM), lambda i, j: (0, i, 0)),
            pl.BlockSpec((HEADS, T, HEAD_DIM), lambda i, j: (0, i, 0)),
        ],
        out_shape=[
            jax.ShapeDtypeStruct((4, NT, GROUP_W, T), BF16),
            jax.ShapeDtypeStruct((HEADS, SEQ, 2 * HEAD_DIM), BF16),
            jax.ShapeDtypeStruct((HEADS, SEQ, HEAD_DIM), BF16),
        ],
        scratch_shapes=[pltpu.VMEM((T, D_MODEL), BF16), pltpu.VMEM((1, LANES), F32),
                        pltpu.VMEM((T, GROUP_W), F32)],
        compiler_params=pltpu.CompilerParams(
            dimension_semantics=("arbitrary", "arbitrary"), vmem_limit_bytes=VMEM_LIMIT),
        name="proj",
    )(x, g1, w_fox, w_diff, wfg, bfg, fqg, fkg, dqg, dkg, cos_t, sin_t)


def _attend(i, scores, allowed, hi_cols, pairs_per_trip, vt_ref, s_refs, cmax_refs, m_ref, acc_ref):
    m_ref[...] = jnp.full_like(m_ref, NEG)
    acc_ref[...] = jnp.zeros_like(acc_ref)

    def vt_ones(kv):
        return jnp.concatenate([vt_ref[kv], jnp.ones((SUM_ROWS, T), BF16)], axis=0)

    def produce(b, kv):
        s = scores(kv, None)
        s_refs[b][...] = s
        cmax_refs[b][...] = jnp.max(s, axis=0, keepdims=True)

    def consume(b, kv):
        m_old = m_ref[...]
        m_new = jnp.maximum(m_old, cmax_refs[b][...])
        p = jnp.exp2(s_refs[b][...] - m_new)
        alpha = jnp.exp2(m_old - m_new)
        acc_ref[...] = alpha * acc_ref[...] + jnp.dot(vt_ones(kv), p.astype(BF16), preferred_element_type=F32)
        m_ref[...] = m_new

    def consume_cols(s, kv, cols):
        load = lambda ref: jnp.concatenate([ref[:, c] for c in cols], axis=1)
        m_old = load(m_ref)
        m_new = jnp.maximum(m_old, jnp.max(s, axis=0, keepdims=True))
        p = jnp.exp2(s - m_new)
        alpha = jnp.exp2(m_old - m_new)
        acc_new = alpha * load(acc_ref) + jnp.dot(vt_ones(kv), p.astype(BF16), preferred_element_type=F32)
        off = 0
        for c in cols:
            part = slice(off, off + c.stop - c.start)
            m_ref[:, c], acc_ref[:, c] = m_new[:, part], acc_new[:, part]
            off = part.stop

    produce(0, 0)

    def pair(kv):
        produce(1, kv + 1)
        consume(0, kv)
        produce(0, kv + 2)
        consume(1, kv + 1)

    def body(a, carry):
        for u in range(pairs_per_trip):
            pair(2 * (pairs_per_trip * a + u))
        return carry

    lax.fori_loop(0, i // pairs_per_trip, body, 0)
    for u in range(pairs_per_trip - 1):
        @pl.when((i % pairs_per_trip) > u)
        def _():
            pair(2 * ((i // pairs_per_trip) * pairs_per_trip + u))
    w_hi = sum(c.stop - c.start for c in hi_cols)
    s_refs[1][:, :w_hi] = jnp.where(allowed(0, hi_cols), scores(2 * i + 1, hi_cols), NEG)
    s_diag = jnp.where(allowed(0, None), s_refs[0][...], NEG)
    s_refs[0][...] = s_diag
    cmax_refs[0][...] = jnp.max(s_diag, axis=0, keepdims=True)
    consume(0, 2 * i)
    consume_cols(s_refs[1][:, :w_hi], 2 * i + 1, hi_cols)


def _fox_kernel(q_ref, k_ref, vt_ref, wdown_ref, wout_ref, o_ref, wdown_bf_ref, wout_bf_ref,
                sa_ref, sb_ref, ca_ref, cb_ref, m_ref, acc_ref):
    wdown_bf_ref[...] = wdown_ref[...].astype(BF16)
    wout_bf_ref[...] = wout_ref[...].astype(BF16)

    def with_bias_rows(q):
        row = lax.broadcasted_iota(jnp.int32, q.shape, 0)
        ones = jnp.where(row < BIAS_TERMS, 1.0, 0.0).astype(BF16)
        return jnp.concatenate([q, ones], axis=0)

    def allowed(key_off, cols):
        width = T if cols else TQ
        key = lax.broadcasted_iota(jnp.int32, (T, width), 0) + key_off
        qry = lax.broadcasted_iota(jnp.int32, (T, width), 1)
        return key <= qry

    hi_cols = [slice(T, TQ)]
    for sub in range(FOX_Q_PER_STEP):
        i = pl.program_id(1) * FOX_Q_PER_STEP + sub
        qb = with_bias_rows(jnp.concatenate([q_ref[2 * sub], q_ref[2 * sub + 1]], axis=1))

        def scores(kv, cols, sub=sub, qb=qb):
            off = pl.multiple_of(kv * T, T)
            rhs = with_bias_rows(q_ref[2 * sub + 1]) if cols else qb
            return jnp.dot(k_ref[pl.ds(off, T), :], rhs, preferred_element_type=F32)

        _attend(i, scores, allowed, hi_cols, FOX_PAIRS_PER_TRIP, vt_ref, (sa_ref, sb_ref), (ca_ref, cb_ref),
                m_ref, acc_ref)
        o = acc_ref[:HEAD_DIM, :] / acc_ref[HEAD_DIM:HEAD_DIM + 1, :]
        o_ref[sub * TQ:(sub + 1) * TQ, :] = o.T.astype(BF16)


def _diff_kernel(q_ref, k_ref, vt_ref, lq1_ref, lk1_ref, lq2_ref, lk2_ref, sg_ref,
                 wup_ref, o_ref, wgate_ref, wval_ref, sa_ref, sb_ref, ca_ref, cb_ref, m_ref, acc_ref):
    wgate_ref[...] = wup_ref[:, :D_FF].astype(BF16)
    wval_ref[...] = wup_ref[:, D_FF:].astype(BF16)

    def split_parts(q):
        part = lax.broadcasted_iota(jnp.int32, q.shape, 0) < DIFF_QK
        zero = jnp.zeros_like(q)
        return jnp.concatenate([jnp.where(part, q, zero), jnp.where(part, zero, q)], axis=1)

    def allowed(key_off, cols):
        width = T if cols else TQ
        key = lax.broadcasted_iota(jnp.int32, (T, 2 * width), 0) + key_off
        qry = lax.broadcasted_iota(jnp.int32, (T, 2 * width), 1) & (width - 1)
        return (key >> CHUNK_SHIFT) <= (qry >> CHUNK_SHIFT)

    hi_cols = [slice(T, TQ), slice(TQ + T, 2 * TQ)]
    lam = (jnp.exp(jnp.sum(lq1_ref[...] * lk1_ref[...], axis=-1, keepdims=True))
           - jnp.exp(jnp.sum(lq2_ref[...] * lk2_ref[...], axis=-1, keepdims=True))
           + LAMBDA_INIT)
    for sub in range(DIFF_Q_PER_STEP):
        i = pl.program_id(1) * DIFF_Q_PER_STEP + sub
        qq = split_parts(jnp.concatenate([q_ref[2 * sub], q_ref[2 * sub + 1]], axis=1))

        def scores(kv, cols, sub=sub, qq=qq):
            off = pl.multiple_of(kv * T, T)
            rhs = split_parts(q_ref[2 * sub + 1]) if cols else qq
            return jnp.dot(k_ref[pl.ds(off, T), :], rhs, preferred_element_type=F32)

        _attend(i, scores, allowed, hi_cols, DIFF_PAIRS_PER_TRIP, vt_ref, (sa_ref, sb_ref), (ca_ref, cb_ref),
                m_ref, acc_ref)
        o_all = acc_ref[:HEAD_DIM, :] / acc_ref[HEAD_DIM:HEAD_DIM + 1, :]
        o = o_all[:, :TQ] - lam * o_all[:, TQ:]
        ms = jnp.mean(o * o, axis=0, keepdims=True)
        o = o * lax.rsqrt(ms + EPS) * (sg_ref[...] * (1.0 - LAMBDA_INIT))
        o_ref[sub * TQ:(sub + 1) * TQ, :] = o.T.astype(BF16)


def _attn_specs(q_seg, v_seg, k_width, q_per_step):
    return [
        pl.BlockSpec((None, 2 * q_per_step, HEAD_DIM, T), lambda h, i: (q_seg, i, h, 0)),
        pl.BlockSpec((None, SEQ, k_width), lambda h, i: (h, 0, 0)),
        pl.BlockSpec((None, NT, HEAD_DIM, T), lambda h, i: (v_seg, 0, h, 0)),
    ]


def _attn_scratch(width):
    return [pltpu.VMEM((T, width), F32), pltpu.VMEM((T, width), F32),
            pltpu.VMEM((1, width), F32), pltpu.VMEM((1, width), F32),
            pltpu.VMEM((1, width), F32), pltpu.VMEM((HEAD_DIM + SUM_ROWS, width), F32)]


def _slab_spec(rows, cols, steps_per_head):
    return pl.BlockSpec((rows, cols), lambda h, i: (h * steps_per_head + i, 0))


def _fox_attn(pt, k_fox, w_down, w_out):
    steps = NQ // FOX_Q_PER_STEP
    wout_slab = _slab_spec(2 * GROUP_W // (HEADS * steps), D_MODEL, steps)
    wdown_slab = _slab_spec(BF16_ROWS * -(-D_FF // (BF16_ROWS * HEADS * steps)), D_MODEL, steps)
    return pl.pallas_call(
        _fox_kernel,
        grid=(HEADS, steps),
        in_specs=_attn_specs(PT_FQ, PT_FV, 2 * HEAD_DIM, FOX_Q_PER_STEP) + [wdown_slab, wout_slab],
        out_specs=[pl.BlockSpec((FOX_Q_PER_STEP * TQ, HEAD_DIM), lambda h, i: (i, h)), wdown_slab, wout_slab],
        out_shape=[jax.ShapeDtypeStruct((SEQ, GROUP_W), BF16),
                   jax.ShapeDtypeStruct((D_FF, D_MODEL), BF16),
                   jax.ShapeDtypeStruct((2 * GROUP_W, D_MODEL), BF16)],
        scratch_shapes=_attn_scratch(TQ),
        compiler_params=pltpu.CompilerParams(
            dimension_semantics=("arbitrary", "arbitrary"), vmem_limit_bytes=VMEM_LIMIT),
        name="fox_attn",
    )(pt, k_fox, pt, w_down, w_out)


def _diff_attn(pt, k_diff, lq1, lk1, lq2, lk2, sg, w_up):
    steps = NQ // DIFF_Q_PER_STEP
    rows = D_MODEL // (HEADS * steps)
    lam_vec = pl.BlockSpec((1, DIFF_QK), lambda h, i: (0, 0))
    return pl.pallas_call(
        _diff_kernel,
        grid=(HEADS, steps),
        in_specs=(_attn_specs(PT_DQ, PT_DV, HEAD_DIM, DIFF_Q_PER_STEP) + [lam_vec] * 4
                  + [pl.BlockSpec((HEAD_DIM, 1), lambda h, i: (0, 0)), _slab_spec(rows, 2 * D_FF, steps)]),
        out_specs=[pl.BlockSpec((DIFF_Q_PER_STEP * TQ, HEAD_DIM), lambda h, i: (i, h)),
                   _slab_spec(rows, D_FF, steps), _slab_spec(rows, D_FF, steps)],
        out_shape=[jax.ShapeDtypeStruct((SEQ, GROUP_W), BF16),
                   jax.ShapeDtypeStruct((D_MODEL, D_FF), BF16),
                   jax.ShapeDtypeStruct((D_MODEL, D_FF), BF16)],
        scratch_shapes=_attn_scratch(2 * TQ),
        compiler_params=pltpu.CompilerParams(
            dimension_semantics=("arbitrary", "arbitrary"), vmem_limit_bytes=VMEM_LIMIT),
        name="diff_attn",
    )(pt, k_diff, pt, lq1, lk1, lq2, lk2, sg, w_up)


def _outproj_kernel(of_ref, od_ref, wo_ref, x_ref, g2_ref, x1_ref, h2_ref):
    mix = (jnp.dot(of_ref[...], wo_ref[0:GROUP_W, :], preferred_element_type=F32)
           + jnp.dot(od_ref[...], wo_ref[GROUP_W:, :], preferred_element_type=F32))
    x1 = x_ref[...] + mix
    x1_ref[...] = x1
    h2_ref[...] = _rms_rows(x1, g2_ref[...]).astype(BF16)


def _outproj(o_f, o_d, wo, x, g2):
    return pl.pallas_call(
        _outproj_kernel,
        grid=(NT,),
        in_specs=[
            pl.BlockSpec((T, GROUP_W), lambda i: (i, 0)),
            pl.BlockSpec((T, GROUP_W), lambda i: (i, 0)),
            pl.BlockSpec((2 * GROUP_W, D_MODEL), lambda i: (0, 0)),
            pl.BlockSpec((T, D_MODEL), lambda i: (i, 0)),
            pl.BlockSpec((1, D_MODEL), lambda i: (0, 0)),
        ],
        out_specs=[pl.BlockSpec((T, D_MODEL), lambda i: (i, 0)),
                   pl.BlockSpec((T, D_MODEL), lambda i: (i, 0))],
        out_shape=[jax.ShapeDtypeStruct((SEQ, D_MODEL), F32),
                   jax.ShapeDtypeStruct((SEQ, D_MODEL), BF16)],
        compiler_params=pltpu.CompilerParams(
            dimension_semantics=("arbitrary",), vmem_limit_bytes=VMEM_LIMIT),
        name="outproj",
    )(o_f, o_d, wo, x, g2)


def _ffn_kernel(h2_ref, halo_ref, wg_ref, wv_ref, cwg_ref, cwv_ref, cbg_ref, cbv_ref, wd_ref, x1_ref,
                out_ref, he_ref, acc_ref, ug_ref, uv_ref):
    i = pl.program_id(0)
    j = pl.program_id(1)

    @pl.when(j == 0)
    def _():
        @pl.when(i == 0)
        def _():
            he_ref[0:HALO, :] = jnp.zeros((HALO, D_MODEL), BF16)

        @pl.when(i > 0)
        def _():
            he_ref[0:HALO, :] = halo_ref[...]

        he_ref[HALO:, :] = h2_ref[...]
        acc_ref[...] = jnp.zeros_like(acc_ref)

    he = he_ref[...]

    def conv(w_ref, cw_ref, cb_ref, u_ref):
        u_ref[...] = jnp.dot(he, w_ref[...], preferred_element_type=F32)
        cw = cw_ref[...]
        u_t = u_ref[HALO:, :]
        u_t1 = u_ref[HALO - 1:HALO - 1 + T, :]
        u_t2 = u_ref[HALO - 2:HALO - 2 + T, :]
        return cw[2:3, :] * u_t + cw[1:2, :] * u_t1 + cw[0:1, :] * u_t2 + cb_ref[...]

    gate = conv(wg_ref, cwg_ref, cbg_ref, ug_ref)
    val = conv(wv_ref, cwv_ref, cbv_ref, uv_ref)
    n_valid = D_FF - j * TF
    act = gate * jax.nn.sigmoid(gate) * val
    act = jnp.where(lax.broadcasted_iota(jnp.int32, act.shape, 1) < n_valid, act, 0.0).astype(BF16)
    wd = wd_ref[...]
    wd = jnp.where(lax.broadcasted_iota(jnp.int32, wd.shape, 0) < n_valid, wd, jnp.zeros_like(wd))
    acc_ref[...] += jnp.dot(act, wd, preferred_element_type=F32)

    @pl.when(j == NF - 1)
    def _():
        out_ref[...] = x1_ref[...] + acc_ref[...]


def _ffn(h2, w_gate, w_val, cw_gate, cw_val, cb_gate, cb_val, wd, x1):
    halo_blocks_per_tile = T // HALO
    return pl.pallas_call(
        _ffn_kernel,
        grid=(NT, NF),
        in_specs=[
            pl.BlockSpec((T, D_MODEL), lambda i, j: (i, 0)),
            pl.BlockSpec((HALO, D_MODEL), lambda i, j: (jnp.maximum(i * halo_blocks_per_tile - 1, 0), 0)),
            pl.BlockSpec((D_MODEL, TF), lambda i, j: (0, j)),
            pl.BlockSpec((D_MODEL, TF), lambda i, j: (0, j)),
            pl.BlockSpec((3, TF), lambda i, j: (0, j)),
            pl.BlockSpec((3, TF), lambda i, j: (0, j)),
            pl.BlockSpec((1, TF), lambda i, j: (0, j)),
            pl.BlockSpec((1, TF), lambda i, j: (0, j)),
            pl.BlockSpec((TF, D_MODEL), lambda i, j: (j, 0)),
            pl.BlockSpec((T, D_MODEL), lambda i, j: (i, 0)),
        ],
        out_specs=pl.BlockSpec((T, D_MODEL), lambda i, j: (i, 0)),
        out_shape=jax.ShapeDtypeStruct((SEQ, D_MODEL), F32),
        scratch_shapes=[pltpu.VMEM((HALO + T, D_MODEL), BF16), pltpu.VMEM((T, D_MODEL), F32),
                        pltpu.VMEM((HALO + T, TF), F32), pltpu.VMEM((HALO + T, TF), F32)],
        compiler_params=pltpu.CompilerParams(
            dimension_semantics=("arbitrary", "arbitrary"), vmem_limit_bytes=VMEM_LIMIT),
        name="ffn",
    )(h2, h2, w_gate, w_val, cw_gate, cw_val, cb_gate, cb_val, wd, x1)


def _rope_tables():
    inv_freq = ROPE_THETA ** (-jnp.arange(0, 2 * ROPE_HALF, 2, dtype=F32) / (2 * ROPE_HALF))
    ang = inv_freq[:, None] * jnp.arange(SEQ, dtype=F32)[None, :]
    return jnp.cos(ang), jnp.sin(ang)


def kernel(x, norm1_g, w_in, b_forget, fox_q_g, fox_k_g, diff_q_g, diff_k_g, lam_q1, lam_k1, lam_q2,
           lam_k2, diff_subln_g, w_out, norm2_g, w_up, conv_w, conv_b, w_down):
    assert x.shape == (1, SEQ, D_MODEL) and w_in.shape[0] == 1
    xs = x[0]
    w = w_in[0]
    n_fox = 3 * GROUP_W
    w_fox = w[:, :n_fox].astype(BF16)
    w_diff = w[:, n_fox + HEADS:].astype(BF16)
    wfg = jnp.pad(w[:, n_fox:n_fox + HEADS], ((0, 0), (0, LANES - HEADS))).astype(BF16)
    bfg = jnp.pad(b_forget, ((0, 0), (0, LANES - HEADS)))
    cos_t, sin_t = _rope_tables()

    pt, k_fox, k_diff = _proj(xs, norm1_g, w_fox, w_diff, wfg, bfg, fox_q_g[0][:, None], fox_k_g,
                              diff_q_g[0][:, None], diff_k_g[0][:, None], cos_t, sin_t)
    o_f, wd_bf, wo_bf = _fox_attn(pt, k_fox, w_down[0], w_out[0])
    o_d, w_gate, w_val = _diff_attn(pt, k_diff, lam_q1, lam_k1, lam_q2, lam_k2, diff_subln_g[0][:, None], w_up[0])
    x1, h2 = _outproj(o_f, o_d, wo_bf, xs, norm2_g)

    cw = conv_w[0]
    out = _ffn(h2, w_gate, w_val, cw[:, :D_FF], cw[:, D_FF:], conv_b[:, :D_FF], conv_b[:, D_FF:], wd_bf, x1)
    return out[None]
```
